```python
import functools
import jax, jax.numpy as jnp
from jax import lax
import numpy as np

D_MODEL = 2048
BATCH = 4
SEQ = 4096
DEPTH = 1
DEC_BATCH = 128
DEC_SEQ = 8
PAST_LEN = 16384
PAGE_SIZE = 128

MLA_HEADS = 16
Q_LORA = 512
KV_LORA = 512
QK_NOPE = 128
QK_ROPE = 64
V_HEAD = 128
MLA_WIDTH = MLA_HEADS * V_HEAD
MLA_SCALE = (QK_NOPE + QK_ROPE) ** -0.5
Q_BLOCK = 128
RET_HEADS = 8
RET_DK = D_MODEL // RET_HEADS
RET_DV = D_MODEL // RET_HEADS
RET_WIDTH = RET_HEADS * RET_DV
RET_K_SCALE = RET_DK ** -0.5
RET_CHUNK = 128
D_FF = ((8 * D_MODEL // 3 + 255) // 256) * 256
ROPE_THETA = 10000.0
EPS = 1e-6
NEG_INF = -1e30
IN_SIZES = (Q_LORA, KV_LORA, QK_ROPE, RET_HEADS * RET_DK, RET_HEADS * RET_DK, RET_WIDTH, RET_WIDTH, MLA_WIDTH, RET_WIDTH)
IN_TOTAL = sum(IN_SIZES)

kernel_name = 'mla_retention_gated_hybrid_step'


def rmsnorm(x, g):
    xf = x.astype(jnp.float32)
    y = xf * lax.rsqrt(jnp.mean(xf * xf, axis=-1, keepdims=True) + EPS)
    return (y * g.astype(jnp.float32)).astype(x.dtype)


def rope(x, pos):
    half = x.shape[-1] // 2
    inv = ROPE_THETA ** (-jnp.arange(half, dtype=jnp.float32) / half)
    ang = pos.astype(jnp.float32)[:, None] * inv[None, :]
    cos = jnp.cos(ang)[None, :, None, :]
    sin = jnp.sin(ang)[None, :, None, :]
    xf = x.astype(jnp.float32)
    x1, x2 = xf[..., :half], xf[..., half:]
    return jnp.concatenate([x1 * cos - x2 * sin, x1 * sin + x2 * cos], axis=-1).astype(x.dtype)


def split_in(proj):
    offs = [int(v) for v in np.cumsum(IN_SIZES)[:-1]]
    return jnp.split(proj, offs, axis=-1)


def mla_project(h_cq, h_ckv, h_kpe, pos, q_norm, kv_norm, w_uq, w_uk):
    cq = rmsnorm(h_cq, q_norm)
    q = jnp.einsum('btr,rhd->bthd', cq, w_uq)
    q_pe = rope(q[..., QK_NOPE:], pos)
    q_lat = jnp.einsum('bthd,hcd->bthc', q[..., :QK_NOPE], w_uk)
    ckv = rmsnorm(h_ckv, kv_norm)
    kpe = rope(h_kpe[:, :, None, :], pos)[:, :, 0, :]
    return q_lat, q_pe, ckv, kpe


def mla_scores(q_lat, q_pe, ckv, kpe):
    s = jnp.einsum('bthc,bkc->bhtk', q_lat, ckv, preferred_element_type=jnp.float32)
    s = s + jnp.einsum('bthr,bkr->bhtk', q_pe, kpe, preferred_element_type=jnp.float32)
    return s * MLA_SCALE


def mla_prompt_attend(q_lat, q_pe, ckv, kpe):
    b, s_len, h, c = q_lat.shape
    nb = s_len // Q_BLOCK
    qlb = q_lat.reshape(b, nb, Q_BLOCK, h, c).swapaxes(0, 1)
    qpb = q_pe.reshape(b, nb, Q_BLOCK, h, QK_ROPE).swapaxes(0, 1)
    kpos = jnp.arange(s_len, dtype=jnp.int32)

    def block(args):
        ql, qp, i = args
        s = mla_scores(ql, qp, ckv, kpe)
        qpos = i * Q_BLOCK + jnp.arange(Q_BLOCK, dtype=jnp.int32)
        s = jnp.where(kpos[None, :] <= qpos[:, None], s, NEG_INF)
        p = jax.nn.softmax(s, axis=-1)
        return jnp.einsum('bhtk,bkc->bthc', p.astype(ckv.dtype), ckv, preferred_element_type=jnp.float32)

    o = lax.map(block, (qlb, qpb, jnp.arange(nb, dtype=jnp.int32)))
    return o.swapaxes(0, 1).reshape(b, s_len, h, c)


def online_update(carry, s, vals):
    m, l, acc = carry
    m_new = jnp.maximum(m, jnp.max(s, axis=-1))
    corr = jnp.exp(m - m_new)
    p = jnp.exp(s - m_new[..., None])
    l = l * corr + jnp.sum(p, axis=-1)
    acc = acc * corr[..., None] + jnp.einsum('bhtk,bkc->bhtc', p, vals.astype(jnp.float32))
    return m_new, l, acc


def mla_sample_attend(q_lat, q_pe, ckv, kpe, cache_ckv, cache_kpe, page_table):
    b, t, h, c = q_lat.shape
    init = (jnp.full((b, h, t), NEG_INF, jnp.float32),
            jnp.zeros((b, h, t), jnp.float32),
            jnp.zeros((b, h, t, c), jnp.float32))

    def page_step(carry, pages):
        kc = cache_ckv[pages]
        kr = cache_kpe[pages]
        return online_update(carry, mla_scores(q_lat, q_pe, kc, kr), kc), None

    carry, _ = lax.scan(page_step, init, page_table.T)
    s_self = mla_scores(q_lat, q_pe, ckv, kpe)
    s_self = jnp.where(jnp.tril(jnp.ones((t, t), dtype=bool)), s_self, NEG_INF)
    _, l, acc = online_update(carry, s_self, ckv)
    return (acc / l[..., None]).swapaxes(1, 2)


def retention_scan(q, k, v, s0, chunk):
    b, t, h, _ = q.shape
    dv = v.shape[-1]
    n = t // chunk
    lg = jnp.log(1.0 - 2.0 ** (-5.0 - jnp.arange(h, dtype=jnp.float32)))
    idx = jnp.arange(chunk, dtype=jnp.float32)
    diff = idx[:, None] - idx[None, :]
    dmat = jnp.where(diff >= 0, jnp.exp(jnp.maximum(diff, 0.0)[None] * lg[:, None, None]), 0.0)
    q_dec = jnp.exp((idx + 1.0)[:, None] * lg[None, :])
    k_dec = jnp.exp((chunk - 1.0 - idx)[:, None] * lg[None, :])
    c_dec = jnp.exp(chunk * lg)

    def to_chunks(a):
        return a.astype(jnp.float32).reshape(b, n, chunk, h, a.shape[-1]).swapaxes(0, 1)

    def step(s, xs):
        qi, ki, vi = xs
        a = jnp.einsum('bihd,bjhd->bhij', qi, ki) * dmat[None]
        o = jnp.einsum('bhij,bjhv->bihv', a, vi) + jnp.einsum('bihd,bhdv->bihv', qi, s) * q_dec[None, :, :, None]
        s = s * c_dec[None, :, None, None] + jnp.einsum('bjhd,bjhv->bhdv', ki * k_dec[None, :, :, None], vi)
        return s, o

    s, o = lax.scan(step, s0.astype(jnp.float32), (to_chunks(q), to_chunks(k), to_chunks(v)))
    return o.swapaxes(0, 1).reshape(b, t, h, dv), s


def retention_branch(r_q, r_k, r_v, r_g, pos, s0, chunk, ret_norm):
    b, t, _ = r_q.shape
    q = rope(r_q.reshape(b, t, RET_HEADS, RET_DK), pos)
    k = rope(r_k.reshape(b, t, RET_HEADS, RET_DK), pos) * RET_K_SCALE
    v = r_v.reshape(b, t, RET_HEADS, RET_DV)
    o, s = retention_scan(q, k, v, s0, chunk)
    mu = jnp.mean(o, axis=-1, keepdims=True)
    var = jnp.mean(jnp.square(o - mu), axis=-1, keepdims=True)
    o = ((o - mu) * lax.rsqrt(var + EPS)).reshape(b, t, RET_WIDTH) * ret_norm.astype(jnp.float32)
    return (jax.nn.silu(r_g.astype(jnp.float32)) * o).astype(r_q.dtype), s


def trunk_layer(x, pos, attend, ret_s0, ret_chunk, p):
    b, t, _ = x.shape
    h = rmsnorm(x, p['norm_mix_pre'])
    c_q, c_kv, k_pe, r_q, r_k, r_v, r_g, g_mla, g_ret = split_in(h @ p['w_in'])
    q_lat, q_pe, ckv, kpe = mla_project(c_q, c_kv, k_pe, pos, p['q_norm'], p['kv_norm'], p['w_uq'], p['w_uk'])
    o_lat = attend(q_lat, q_pe, ckv, kpe).astype(x.dtype)
    o_mla = jnp.einsum('bthc,hcv->bthv', o_lat, p['w_uv']).reshape(b, t, MLA_WIDTH)
    o_ret, s_new = retention_branch(r_q, r_k, r_v, r_g, pos, ret_s0, ret_chunk, p['ret_norm'])
    mixed = jax.nn.sigmoid(g_mla) * o_mla + jax.nn.sigmoid(g_ret) * o_ret
    x = x + rmsnorm(mixed @ p['w_out'], p['norm_mix_post'])
    f = rmsnorm(x, p['norm_ffn_pre'])
    f = (jax.nn.silu(f @ p['w_gate']) * (f @ p['w_up'])) @ p['w_down']
    x = x + rmsnorm(f, p['norm_ffn_post'])
    return x, ckv, kpe, s_new


def setup_inputs(seed: int = 0) -> dict:
    key = jax.random.key(seed)
    ks = jax.random.split(key, 24)
    n_pages = PAST_LEN // PAGE_SIZE
    n_used = DEC_BATCH * n_pages
    n_phys = (n_used * 5 + 3) // 4

    def nrm(k, shape, scale):
        return jax.random.normal(k, shape, jnp.float32) * scale

    def gain(k, n):
        return 1.0 + 0.02 * jax.random.normal(k, (DEPTH, n), jnp.float32)

    page_table = jax.random.permutation(ks[0], n_phys)[:n_used].astype(jnp.int32).reshape(DEC_BATCH, n_pages)
    return {
        'x_prompt': nrm(ks[1], (BATCH, SEQ, D_MODEL), 1.0),
        'x_sample': nrm(ks[2], (DEC_BATCH, DEC_SEQ, D_MODEL), 1.0),
        'cache_kv_latent': nrm(ks[3], (DEPTH, n_phys, PAGE_SIZE, KV_LORA), 1.0),
        'cache_k_rope': nrm(ks[4], (DEPTH, n_phys, PAGE_SIZE, QK_ROPE), 1.0),
        'page_table': page_table,
        'state_retention': nrm(ks[5], (DEPTH, DEC_BATCH, RET_HEADS, RET_DK, RET_DV), 0.5),
        'w_in': nrm(ks[6], (DEPTH, D_MODEL, IN_TOTAL), D_MODEL ** -0.5),
        'q_norm': gain(ks[7], Q_LORA),
        'kv_norm': gain(ks[8], KV_LORA),
        'w_uq': nrm(ks[9], (DEPTH, Q_LORA, MLA_HEADS, QK_NOPE + QK_ROPE), Q_LORA ** -0.5),
        'w_uk': nrm(ks[10], (DEPTH, MLA_HEADS, KV_LORA, QK_NOPE), KV_LORA ** -0.5),
        'w_uv': nrm(ks[11], (DEPTH, MLA_HEADS, KV_LORA, V_HEAD), KV_LORA ** -0.5),
        'ret_norm': gain(ks[12], RET_WIDTH),
        'w_out': nrm(ks[13], (DEPTH, D_MODEL, D_MODEL), D_MODEL ** -0.5),
        'norm_mix_pre': gain(ks[14], D_MODEL),
        'norm_mix_post': gain(ks[15], D_MODEL),
        'norm_ffn_pre': gain(ks[16], D_MODEL),
        'norm_ffn_post': gain(ks[17], D_MODEL),
        'w_gate': nrm(ks[18], (DEPTH, D_MODEL, D_FF), D_MODEL ** -0.5),
        'w_up': nrm(ks[19], (DEPTH, D_MODEL, D_FF), D_MODEL ** -0.5),
        'w_down': nrm(ks[20], (DEPTH, D_FF, D_MODEL), D_FF ** -0.5),
    }


def reference(x_prompt, x_sample, cache_kv_latent, cache_k_rope, page_table, state_retention,
              w_in, q_norm, kv_norm, w_uq, w_uk, w_uv, ret_norm, w_out,
              norm_mix_pre, norm_mix_post, norm_ffn_pre, norm_ffn_post, w_gate, w_up, w_down):
    b, s_len, _ = x_prompt.shape
    t_s = x_sample.shape[1]
    past_len = page_table.shape[1] * PAGE_SIZE
    pos_p = jnp.arange(s_len, dtype=jnp.int32)
    pos_s = past_len + jnp.arange(t_s, dtype=jnp.int32)
    yp, ys = x_prompt, x_sample
    ckv_p, kpe_p, ret_p, ckv_s, kpe_s, ret_s = [], [], [], [], [], []
    for l in range(DEPTH):
        p = {'w_in': w_in[l], 'q_norm': q_norm[l], 'kv_norm': kv_norm[l], 'w_uq': w_uq[l],
             'w_uk': w_uk[l], 'w_uv': w_uv[l], 'ret_norm': ret_norm[l], 'w_out': w_out[l],
             'norm_mix_pre': norm_mix_pre[l], 'norm_mix_post': norm_mix_post[l],
             'norm_ffn_pre': norm_ffn_pre[l], 'norm_ffn_post': norm_ffn_post[l],
             'w_gate': w_gate[l], 'w_up': w_up[l], 'w_down': w_down[l]}
        zero_state = jnp.zeros((b, RET_HEADS, RET_DK, RET_DV), jnp.float32)
        yp, c1, k1, s1 = trunk_layer(yp, pos_p, mla_prompt_attend, zero_state, RET_CHUNK, p)
        attend_s = functools.partial(mla_sample_attend, cache_ckv=cache_kv_latent[l],
                                     cache_kpe=cache_k_rope[l], page_table=page_table)
        ys, c2, k2, s2 = trunk_layer(ys, pos_s, attend_s, state_retention[l], t_s, p)
        ckv_p.append(c1)
        kpe_p.append(k1)
        ret_p.append(s1.astype(x_prompt.dtype))
        ckv_s.append(c2)
        kpe_s.append(k2)
        ret_s.append(s2.astype(state_retention.dtype))
    return (yp, ys, jnp.stack(ckv_p), jnp.stack(kpe_p), jnp.stack(ret_p),
            jnp.stack(ckv_s), jnp.stack(kpe_s), jnp.stack(ret_s))
```

```python
import functools
import math

import jax
import jax.numpy as jnp
from jax import lax
from jax.experimental import pallas as pl
from jax.experimental.pallas import tpu as pltpu

F32 = jnp.float32
BF16 = jnp.bfloat16

D_MODEL = 2048
PAGE_SIZE = 128
MLA_HEADS = 16
Q_LORA = 512
KV_LORA = 512
QK_NOPE = 128
QK_ROPE = 64
V_HEAD = 128
MLA_WIDTH = MLA_HEADS * V_HEAD
MLA_SCALE = (QK_NOPE + QK_ROPE) ** -0.5
K_CAT = KV_LORA + QK_ROPE
RET_HEADS = 8
RET_DK = D_MODEL // RET_HEADS
RET_DV = D_MODEL // RET_HEADS
RET_WIDTH = RET_HEADS * RET_DV
RET_K_SCALE = RET_DK ** -0.5
RET_CHUNK = 128
ROPE_THETA = 10000.0
EPS = 1e-6
NEG_INF = -1e30
IN_SIZES = (Q_LORA, KV_LORA, QK_ROPE, RET_HEADS * RET_DK, RET_HEADS * RET_DK,
            RET_WIDTH, RET_WIDTH, MLA_WIDTH, RET_WIDTH)

LANES = 128
VMEM_LIMIT = 56 * 1024 * 1024


def _params(*sem):
    return pltpu.CompilerParams(dimension_semantics=sem, vmem_limit_bytes=VMEM_LIMIT)


def _sigmoid(x):
    return 1.0 / (1.0 + jnp.exp(-x))


def _dot(a, b):
    return jnp.dot(a, b, preferred_element_type=F32)


def _dot_nt(a, b):
    return lax.dot_general(a, b, (((1,), (1,)), ((), ())), preferred_element_type=F32)


def _dot_tn(a, b):
    return lax.dot_general(a, b, (((0,), (0,)), ((), ())), preferred_element_type=F32)


def _rmsnorm_kernel(x_ref, g_ref, o_ref):
    x = x_ref[...]
    ms = jnp.mean(x * x, axis=-1, keepdims=True)
    o_ref[...] = (x * lax.rsqrt(ms + EPS) * g_ref[...]).astype(o_ref.dtype)


def rmsnorm_rows(x, g, tm):
    m, d = x.shape
    return pl.pallas_call(
        _rmsnorm_kernel,
        grid=(m // tm,),
        in_specs=[pl.BlockSpec((tm, d), lambda i: (i, 0)),
                  pl.BlockSpec((1, d), lambda i: (0, 0))],
        out_specs=pl.BlockSpec((tm, d), lambda i: (i, 0)),
        out_shape=jax.ShapeDtypeStruct((m, d), BF16),
        compiler_params=_params("parallel"),
        name="rmsnorm_rows",
    )(x, g.reshape(1, d))


def _mm_kernel(a_ref, w_ref, o_ref):
    o_ref[...] = _dot(a_ref[...], w_ref[...]).astype(o_ref.dtype)


def matmul(a, w, out_dtype, tm, tn):
    m, k = a.shape
    n = w.shape[1]
    return pl.pallas_call(
        _mm_kernel,
        grid=(m // tm, n // tn),
        in_specs=[pl.BlockSpec((tm, k), lambda i, j: (i, 0)),
                  pl.BlockSpec((k, tn), lambda i, j: (0, j))],
        out_specs=pl.BlockSpec((tm, tn), lambda i, j: (i, j)),
        out_shape=jax.ShapeDtypeStruct((m, n), out_dtype),
        compiler_params=_params("parallel", "parallel"),
        name="matmul",
    )(a, w)


def _mm_rope_kernel(a_ref, w_ref, cos_ref, sin_ref, o_ref, *, n_q_tiles, heads_per_tile):
    acc = _dot(a_ref[...], w_ref[...])
    c = cos_ref[...]
    s = sin_ref[...]
    scale = jnp.where(pl.program_id(1) >= n_q_tiles, RET_K_SCALE, 1.0).astype(F32)
    half = RET_DK // 2
    for h in range(heads_per_tile):
        lo = h * RET_DK
        x1 = acc[:, lo:lo + half]
        x2 = acc[:, lo + half:lo + RET_DK]
        o_ref[:, lo:lo + half] = ((x1 * c - x2 * s) * scale).astype(o_ref.dtype)
        o_ref[:, lo + half:lo + RET_DK] = ((x1 * s + x2 * c) * scale).astype(o_ref.dtype)


def matmul_rope(a, w, cos, sin, out_dtype, tm, tn):
    m, k = a.shape
    n = w.shape[1]
    pos_blocks = cos.shape[0] // tm
    kern = functools.partial(_mm_rope_kernel, n_q_tiles=(n // 2) // tn,
                             heads_per_tile=tn // RET_DK)
    return pl.pallas_call(
        kern,
        grid=(m // tm, n // tn),
        in_specs=[pl.BlockSpec((tm, k), lambda i, j: (i, 0)),
                  pl.BlockSpec((k, tn), lambda i, j: (0, j)),
                  pl.BlockSpec((tm, RET_DK // 2), lambda i, j: (i % pos_blocks, 0)),
                  pl.BlockSpec((tm, RET_DK // 2), lambda i, j: (i % pos_blocks, 0))],
        out_specs=pl.BlockSpec((tm, tn), lambda i, j: (i, j)),
        out_shape=jax.ShapeDtypeStruct((m, n), out_dtype),
        compiler_params=_params("parallel", "parallel"),
        name="matmul_rope",
    )(a, w, cos, sin)


def _latent_kernel(a_ref, w_ref, qn_ref, kvn_ref, cos_ref, sin_ref,
                   cq_ref, ckv_ref, kpe_ref, kcat_ref):
    acc = _dot(a_ref[...], w_ref[...])
    cq = acc[:, :Q_LORA]
    cq = cq * lax.rsqrt(jnp.mean(cq * cq, axis=-1, keepdims=True) + EPS) * qn_ref[...]
    cq_ref[...] = cq.astype(cq_ref.dtype)
    ckv = acc[:, Q_LORA:Q_LORA + KV_LORA]
    ckv = ckv * lax.rsqrt(jnp.mean(ckv * ckv, axis=-1, keepdims=True) + EPS) * kvn_ref[...]
    ckv_ref[...] = ckv
    base = Q_LORA + KV_LORA
    x = acc[:, base:base + LANES]
    xs = acc[:, base + LANES:base + 2 * LANES]
    kpe = (x * cos_ref[...] + xs * sin_ref[...])[:, :QK_ROPE]
    kpe_ref[...] = kpe
    kcat_ref[:, :KV_LORA] = ckv.astype(kcat_ref.dtype)
    kcat_ref[:, KV_LORA:K_CAT] = kpe.astype(kcat_ref.dtype)


def latent_proj(a, w, q_norm, kv_norm, cos, sin, tm):
    m, k = a.shape
    n = w.shape[1]
    pos_blocks = cos.shape[0] // tm
    row = lambda i: (i, 0)
    return pl.pallas_call(
        _latent_kernel,
        grid=(m // tm,),
        in_specs=[pl.BlockSpec((tm, k), row),
                  pl.BlockSpec((k, n), lambda i: (0, 0)),
                  pl.BlockSpec((1, Q_LORA), lambda i: (0, 0)),
                  pl.BlockSpec((1, KV_LORA), lambda i: (0, 0)),
                  pl.BlockSpec((tm, LANES), lambda i: (i % pos_blocks, 0)),
                  pl.BlockSpec((tm, LANES), lambda i: (i % pos_blocks, 0))],
        out_specs=[pl.BlockSpec((tm, Q_LORA), row),
                   pl.BlockSpec((tm, KV_LORA), row),
                   pl.BlockSpec((tm, QK_ROPE), row),
                   pl.BlockSpec((tm, K_CAT), row)],
        out_shape=[jax.ShapeDtypeStruct((m, Q_LORA), BF16),
                   jax.ShapeDtypeStruct((m, KV_LORA), F32),
                   jax.ShapeDtypeStruct((m, QK_ROPE), F32),
                   jax.ShapeDtypeStruct((m, K_CAT), BF16)],
        compiler_params=_params("parallel"),
        name="latent_proj",
    )(a, w, q_norm.reshape(1, Q_LORA), kv_norm.reshape(1, KV_LORA), cos, sin)


def _query_kernel(cq_ref, wq_ref, wuk_ref, cos_ref, sin_ref, o_ref):
    q = _dot(cq_ref[...], wq_ref[...])
    for h in range(MLA_HEADS):
        qn = q[:, h * QK_NOPE:(h + 1) * QK_NOPE].astype(BF16)
        o_ref[h, :, :KV_LORA] = _dot(qn, wuk_ref[h]).astype(o_ref.dtype)
    c = cos_ref[...]
    s = sin_ref[...]
    rope0 = MLA_HEADS * QK_NOPE
    swap0 = rope0 + MLA_HEADS * QK_ROPE
    for p in range(MLA_HEADS * QK_ROPE // LANES):
        x = q[:, rope0 + p * LANES:rope0 + (p + 1) * LANES]
        xs = q[:, swap0 + p * LANES:swap0 + (p + 1) * LANES]
        qp = x * c + xs * s
        o_ref[2 * p, :, KV_LORA:K_CAT] = qp[:, :QK_ROPE].astype(o_ref.dtype)
        o_ref[2 * p + 1, :, KV_LORA:K_CAT] = qp[:, QK_ROPE:].astype(o_ref.dtype)


def query_proj(cq, wq, wuk_t, cos, sin, out_dtype, tm):
    m = cq.shape[0]
    pos_blocks = cos.shape[0] // tm
    return pl.pallas_call(
        _query_kernel,
        grid=(m // tm,),
        in_specs=[pl.BlockSpec((tm, Q_LORA), lambda i: (i, 0)),
                  pl.BlockSpec(wq.shape, lambda i: (0, 0)),
                  pl.BlockSpec(wuk_t.shape, lambda i: (0, 0, 0)),
                  pl.BlockSpec((tm, LANES), lambda i: (i % pos_blocks, 0)),
                  pl.BlockSpec((tm, LANES), lambda i: (i % pos_blocks, 0))],
        out_specs=pl.BlockSpec((MLA_HEADS, tm, K_CAT), lambda i: (0, i, 0)),
        out_shape=jax.ShapeDtypeStruct((MLA_HEADS, m, K_CAT), out_dtype),
        compiler_params=_params("parallel"),
        name="query_proj",
    )(cq, wq, wuk_t, cos, sin)


def _online_softmax_step(s, v, m_ref, l_ref, acc_ref):
    m_prev = m_ref[...]
    m_new = jnp.maximum(m_prev, jnp.max(s, axis=-1, keepdims=True))
    corr = jnp.exp(m_prev - m_new)
    p = jnp.exp(s - m_new)
    l_ref[...] = l_ref[...] * corr + jnp.sum(p, axis=-1, keepdims=True)
    acc_ref[...] = acc_ref[...] * corr + _dot(p.astype(BF16), v)
    m_ref[...] = m_new


def _prompt_attn_kernel(q_ref, k_ref, o_ref, m_ref, l_ref, acc_ref, *, tq, tk):
    i = pl.program_id(1)
    j = pl.program_id(2)
    rows = MLA_HEADS * tq

    @pl.when(j == 0)
    def _():
        m_ref[...] = jnp.full(m_ref.shape, NEG_INF, F32)
        l_ref[...] = jnp.zeros(l_ref.shape, F32)
        acc_ref[...] = jnp.zeros(acc_ref.shape, F32)

    def step(masked):
        q = q_ref[...].reshape(rows, K_CAT)
        k = k_ref[...]
        s = _dot_nt(q, k) * MLA_SCALE
        if masked:
            qpos = i * tq + lax.broadcasted_iota(jnp.int32, (tq, tk), 0)
            kpos = j * tk + lax.broadcasted_iota(jnp.int32, (tq, tk), 1)
            keep = (kpos <= qpos)[None]
            s = jnp.where(keep, s.reshape(MLA_HEADS, tq, tk), NEG_INF).reshape(rows, tk)
        _online_softmax_step(s, k[:, :KV_LORA], m_ref, l_ref, acc_ref)

    first_key = j * tk
    last_key = first_key + tk - 1
    first_q = i * tq
    last_q = first_q + tq - 1

    @pl.when(last_key <= first_q)
    def _():
        step(False)

    @pl.when((last_key > first_q) & (first_key <= last_q))
    def _():
        step(True)

    @pl.when(j == pl.num_programs(2) - 1)
    def _():
        o = acc_ref[...] / l_ref[...]
        o_ref[...] = o.reshape(MLA_HEADS, tq, KV_LORA).astype(o_ref.dtype)


def prompt_attention(q_hm, kcat, batch, seq, tq, tk):
    nq = seq // tq
    nk = seq // tk
    rows = MLA_HEADS * tq

    def k_index(b, i, j):
        last_needed = (i * tq + tq - 1) // tk
        return (b * nk + jnp.minimum(j, last_needed), 0)

    kern = functools.partial(_prompt_attn_kernel, tq=tq, tk=tk)
    return pl.pallas_call(
        kern,
        grid=(batch, nq, nk),
        in_specs=[pl.BlockSpec((MLA_HEADS, tq, K_CAT), lambda b, i, j: (0, b * nq + i, 0)),
                  pl.BlockSpec((tk, K_CAT), k_index)],
        out_specs=pl.BlockSpec((MLA_HEADS, tq, KV_LORA), lambda b, i, j: (0, b * nq + i, 0)),
        out_shape=jax.ShapeDtypeStruct((MLA_HEADS, batch * seq, KV_LORA), BF16),
        scratch_shapes=[pltpu.VMEM((rows, 1), F32),
                        pltpu.VMEM((rows, 1), F32),
                        pltpu.VMEM((rows, KV_LORA), F32)],
        compiler_params=_params("parallel", "parallel", "arbitrary"),
        name="prompt_attention",
    )(q_hm, kcat)


def _sample_attn_kernel(pt_ref, q_ref, *refs, pages_per_step, t_new):
    del pt_ref
    ckv_refs = refs[:pages_per_step]
    kpe_refs = refs[pages_per_step:2 * pages_per_step]
    (ckv_new_ref, kpe_new_ref, o_ref,
     kbuf_ref, knew_ref, m_ref, l_ref, acc_ref) = refs[2 * pages_per_step:]
    j = pl.program_id(1)
    rows = MLA_HEADS * t_new

    @pl.when(j == 0)
    def _():
        m_ref[...] = jnp.full(m_ref.shape, NEG_INF, F32)
        l_ref[...] = jnp.zeros(l_ref.shape, F32)
        acc_ref[...] = jnp.zeros(acc_ref.shape, F32)

    q = q_ref[...].reshape(rows, K_CAT).astype(BF16)
    for u in range(pages_per_step):
        kbuf_ref[u * PAGE_SIZE:(u + 1) * PAGE_SIZE, :KV_LORA] = ckv_refs[u][...].astype(BF16)
        kbuf_ref[u * PAGE_SIZE:(u + 1) * PAGE_SIZE, KV_LORA:K_CAT] = kpe_refs[u][...].astype(BF16)
    k = kbuf_ref[...]
    s = _dot_nt(q, k) * MLA_SCALE
    _online_softmax_step(s, k[:, :KV_LORA], m_ref, l_ref, acc_ref)

    @pl.when(j == pl.num_programs(1) - 1)
    def _():
        knew_ref[...] = jnp.zeros(knew_ref.shape, F32)
        knew_ref[:t_new, :KV_LORA] = ckv_new_ref[...]
        knew_ref[:t_new, KV_LORA:K_CAT] = kpe_new_ref[...]
        kn = knew_ref[...].astype(BF16)
        sn = _dot_nt(q, kn) * MLA_SCALE
        qpos = lax.broadcasted_iota(jnp.int32, (t_new, PAGE_SIZE), 0)
        kpos = lax.broadcasted_iota(jnp.int32, (t_new, PAGE_SIZE), 1)
        keep = (kpos <= qpos)[None]
        sn = jnp.where(keep, sn.reshape(MLA_HEADS, t_new, PAGE_SIZE), NEG_INF)
        _online_softmax_step(sn.reshape(rows, PAGE_SIZE), kn[:, :KV_LORA], m_ref, l_ref, acc_ref)
        o = acc_ref[...] / l_ref[...]
        o_ref[...] = o.reshape(MLA_HEADS, t_new, KV_LORA).astype(o_ref.dtype)


def sample_attention(q_hm, cache_ckv, cache_kpe, page_table, ckv_new, kpe_new, t_new, pages_per_step):
    dec_batch, n_pages = page_table.shape
    n_steps = n_pages // pages_per_step
    rows = MLA_HEADS * t_new

    def page_spec(width, u):
        return pl.BlockSpec((None, PAGE_SIZE, width),
                            lambda b, j, pt: (pt[b, j * pages_per_step + u], 0, 0))

    in_specs = [pl.BlockSpec((MLA_HEADS, t_new, K_CAT), lambda b, j, pt: (0, b, 0))]
    in_specs += [page_spec(KV_LORA, u) for u in range(pages_per_step)]
    in_specs += [page_spec(QK_ROPE, u) for u in range(pages_per_step)]
    in_specs += [pl.BlockSpec((t_new, KV_LORA), lambda b, j, pt: (b, 0)),
                 pl.BlockSpec((t_new, QK_ROPE), lambda b, j, pt: (b, 0))]
    kern = functools.partial(_sample_attn_kernel, pages_per_step=pages_per_step, t_new=t_new)
    grid_spec = pltpu.PrefetchScalarGridSpec(
        num_scalar_prefetch=1,
        grid=(dec_batch, n_steps),
        in_specs=in_specs,
        out_specs=pl.BlockSpec((MLA_HEADS, t_new, KV_LORA), lambda b, j, pt: (0, b, 0)),
        scratch_shapes=[pltpu.VMEM((pages_per_step * PAGE_SIZE, K_CAT), BF16),
                        pltpu.VMEM((PAGE_SIZE, K_CAT), F32),
                        pltpu.VMEM((rows, 1), F32),
                        pltpu.VMEM((rows, 1), F32),
                        pltpu.VMEM((rows, KV_LORA), F32)],
    )
    args = [page_table, q_hm] + [cache_ckv] * pages_per_step + [cache_kpe] * pages_per_step
    args += [ckv_new, kpe_new]
    return pl.pallas_call(
        kern,
        grid_spec=grid_spec,
        out_shape=jax.ShapeDtypeStruct((MLA_HEADS, dec_batch * t_new, KV_LORA), F32),
        compiler_params=_params("parallel", "arbitrary"),
        name="sample_attention",
    )(*args)


def _uv_kernel(o_ref, w_ref, out_ref):
    out_ref[...] = _dot(o_ref[...].astype(BF16), w_ref[...])


def value_up_proj(o_lat, w_uv, tm):
    heads, m, _ = o_lat.shape
    return pl.pallas_call(
        _uv_kernel,
        grid=(m // tm, heads),
        in_specs=[pl.BlockSpec((None, tm, KV_LORA), lambda i, h: (h, i, 0)),
                  pl.BlockSpec((None, KV_LORA, V_HEAD), lambda i, h: (h, 0, 0))],
        out_specs=pl.BlockSpec((tm, V_HEAD), lambda i, h: (i, h)),
        out_shape=jax.ShapeDtypeStruct((m, heads * V_HEAD), F32),
        compiler_params=_params("parallel", "parallel"),
        name="value_up_proj",
    )(o_lat, w_uv)


def _retention_kernel(*refs, chunk, has_s0):
    if has_s0:
        (q_ref, k_ref, v_ref, rg_ref, gm_ref, gr_ref, om_ref, rn_ref, s0_ref,
         mix_ref, sout_ref, s_ref) = refs
    else:
        (q_ref, k_ref, v_ref, rg_ref, gm_ref, gr_ref, om_ref, rn_ref,
         mix_ref, sout_ref, s_ref) = refs
    c = pl.program_id(1)

    @pl.when(c == 0)
    def _():
        if has_s0:
            s_ref[...] = s0_ref[0]
        else:
            s_ref[...] = jnp.zeros(s_ref.shape, F32)

    ii = lax.broadcasted_iota(jnp.int32, (chunk, chunk), 0)
    jj = lax.broadcasted_iota(jnp.int32, (chunk, chunk), 1)
    diff = (ii - jj).astype(F32)
    idx = lax.broadcasted_iota(jnp.int32, (chunk, 1), 0).astype(F32)
    for h in range(RET_HEADS):
        lg = math.log(1.0 - 2.0 ** (-5.0 - h))
        dmat = jnp.where(diff >= 0, jnp.exp(jnp.maximum(diff, 0.0) * lg), 0.0)
        q_dec = jnp.exp((idx + 1.0) * lg)
        k_dec = jnp.exp((chunk - 1.0 - idx) * lg)
        c_dec = math.exp(chunk * lg)
        sl = slice(h * RET_DK, (h + 1) * RET_DK)
        q = q_ref[:, sl].astype(BF16)
        k = k_ref[:, sl]
        v = v_ref[:, sl].astype(BF16)
        state = s_ref[h]
        a = _dot_nt(q, k.astype(BF16)) * dmat
        o = _dot(a.astype(BF16), v) + _dot(q, state.astype(BF16)) * q_dec
        kd = (k.astype(F32) * k_dec).astype(BF16)
        s_ref[h] = state * c_dec + _dot_tn(kd, v)
        mu = jnp.mean(o, axis=-1, keepdims=True)
        d = o - mu
        var = jnp.mean(d * d, axis=-1, keepdims=True)
        on = d * lax.rsqrt(var + EPS) * rn_ref[:, sl]
        rg = rg_ref[:, sl]
        o_ret = rg * _sigmoid(rg) * on
        mixed = _sigmoid(gm_ref[:, sl]) * om_ref[:, sl] + _sigmoid(gr_ref[:, sl]) * o_ret
        mix_ref[:, sl] = mixed.astype(mix_ref.dtype)

    @pl.when(c == pl.num_programs(1) - 1)
    def _():
        sout_ref[0] = s_ref[...]


def retention_mix(qk, v, gates, o_mla, ret_norm, s0, batch, n_chunks, chunk, mix_dtype):
    m = v.shape[0]
    has_s0 = s0 is not None
    row = lambda col: (lambda b, c: (b * n_chunks + c, col))
    blk = lambda col: pl.BlockSpec((chunk, RET_WIDTH), row(col))
    state_spec = pl.BlockSpec((1, RET_HEADS, RET_DK, RET_DV), lambda b, c: (b, 0, 0, 0))
    in_specs = [blk(0), blk(1), blk(0), blk(0), blk(1), blk(2), blk(0),
                pl.BlockSpec((1, RET_WIDTH), lambda b, c: (0, 0))]
    args = [qk, qk, v, gates, gates, gates, o_mla, ret_norm.reshape(1, RET_WIDTH)]
    if has_s0:
        in_specs.append(state_spec)
        args.append(s0)
    kern = functools.partial(_retention_kernel, chunk=chunk, has_s0=has_s0)
    return pl.pallas_call(
        kern,
        grid=(batch, n_chunks),
        in_specs=in_specs,
        out_specs=[blk(0), state_spec],
        out_shape=[jax.ShapeDtypeStruct((m, RET_WIDTH), mix_dtype),
                   jax.ShapeDtypeStruct((batch, RET_HEADS, RET_DK, RET_DV), F32)],
        scratch_shapes=[pltpu.VMEM((RET_HEADS, RET_DK, RET_DV), F32)],
        compiler_params=_params("parallel", "arbitrary"),
        name="retention_mix",
    )(*args)


def _out_proj_kernel(mix_ref, w_ref, x_ref, gpost_ref, gpre_ref, x1_ref, f_ref):
    y = _dot(mix_ref[...].astype(BF16), w_ref[...])
    y = y * lax.rsqrt(jnp.mean(y * y, axis=-1, keepdims=True) + EPS) * gpost_ref[...]
    x1 = x_ref[...] + y
    x1_ref[...] = x1
    f = x1 * lax.rsqrt(jnp.mean(x1 * x1, axis=-1, keepdims=True) + EPS) * gpre_ref[...]
    f_ref[...] = f.astype(f_ref.dtype)


def out_proj(mixed, w_out, x, g_post, g_pre, tm):
    m, d = x.shape
    row = lambda i: (i, 0)
    vec = pl.BlockSpec((1, d), lambda i: (0, 0))
    return pl.pallas_call(
        _out_proj_kernel,
        grid=(m // tm,),
        in_specs=[pl.BlockSpec((tm, d), row), pl.BlockSpec((d, d), lambda i: (0, 0)),
                  pl.BlockSpec((tm, d), row), vec, vec],
        out_specs=[pl.BlockSpec((tm, d), row), pl.BlockSpec((tm, d), row)],
        out_shape=[jax.ShapeDtypeStruct((m, d), F32), jax.ShapeDtypeStruct((m, d), BF16)],
        compiler_params=_params("parallel"),
        name="out_proj",
    )(mixed, w_out, x, g_post.reshape(1, d), g_pre.reshape(1, d))


def _ffn_up_kernel(f_ref, wg_ref, wu_ref, o_ref):
    f = f_ref[...]
    g = _dot(f, wg_ref[...])
    u = _dot(f, wu_ref[...])
    o_ref[...] = (g * _sigmoid(g) * u).astype(o_ref.dtype)


def ffn_up(f, w_gate, w_up, tm, tn):
    m, d = f.shape
    n = w_gate.shape[1]
    return pl.pallas_call(
        _ffn_up_kernel,
        grid=(m // tm, n // tn),
        in_specs=[pl.BlockSpec((tm, d), lambda i, j: (i, 0)),
                  pl.BlockSpec((d, tn), lambda i, j: (0, j)),
                  pl.BlockSpec((d, tn), lambda i, j: (0, j))],
        out_specs=pl.BlockSpec((tm, tn), lambda i, j: (i, j)),
        out_shape=jax.ShapeDtypeStruct((m, n), BF16),
        compiler_params=_params("parallel", "parallel"),
        name="ffn_up",
    )(f, w_gate, w_up)


def _ffn_down_kernel(a_ref, w_ref, x_ref, g_ref, o_ref, acc_ref):
    kk = pl.program_id(1)

    @pl.when(kk == 0)
    def _():
        acc_ref[...] = jnp.zeros(acc_ref.shape, F32)

    acc_ref[...] += _dot(a_ref[...], w_ref[...])

    @pl.when(kk == pl.num_programs(1) - 1)
    def _():
        y = acc_ref[...]
        y = y * lax.rsqrt(jnp.mean(y * y, axis=-1, keepdims=True) + EPS) * g_ref[...]
        o_ref[...] = x_ref[...] + y


def ffn_down(act, w_down, x1, g_post, tm, tk):
    m, dff = act.shape
    d = w_down.shape[1]
    return pl.pallas_call(
        _ffn_down_kernel,
        grid=(m // tm, dff // tk),
        in_specs=[pl.BlockSpec((tm, tk), lambda i, k: (i, k)),
                  pl.BlockSpec((tk, d), lambda i, k: (k, 0)),
                  pl.BlockSpec((tm, d), lambda i, k: (i, 0)),
                  pl.BlockSpec((1, d), lambda i, k: (0, 0))],
        out_specs=pl.BlockSpec((tm, d), lambda i, k: (i, 0)),
        out_shape=jax.ShapeDtypeStruct((m, d), F32),
        scratch_shapes=[pltpu.VMEM((tm, d), F32)],
        compiler_params=_params("parallel", "arbitrary"),
        name="ffn_down",
    )(act, w_down, x1, g_post.reshape(1, d))


def _rope_tables(pos, half):
    inv = ROPE_THETA ** (-jnp.arange(half, dtype=F32) / half)
    ang = pos.astype(F32)[:, None] * inv[None, :]
    return jnp.cos(ang), jnp.sin(ang)


def _tile_rows(table, rows):
    reps = max(1, rows // table.shape[0])
    return jnp.tile(table, (reps, 1))


def _prep_weights(w_in, w_uq, w_uk, w_uv, w_out, w_gate, w_up, w_down):
    offs = [0]
    for sz in IN_SIZES:
        offs.append(offs[-1] + sz)
    col = lambda a, b: w_in[:, offs[a]:offs[b]]
    k_pe = col(2, 3)
    half = QK_ROPE // 2
    k_pe_swap = jnp.concatenate([k_pe[:, half:], k_pe[:, :half]], axis=1)
    zpad = jnp.zeros((w_in.shape[0], LANES - QK_ROPE), w_in.dtype)
    w_lat = jnp.concatenate([col(0, 2), k_pe, zpad, k_pe_swap, zpad], axis=1)
    q_nope = w_uq[:, :, :QK_NOPE].reshape(Q_LORA, MLA_HEADS * QK_NOPE)
    q_rope = w_uq[:, :, QK_NOPE:]
    q_rope_swap = jnp.concatenate([q_rope[:, :, half:], q_rope[:, :, :half]], axis=2)
    w_q = jnp.concatenate([q_nope, q_rope.reshape(Q_LORA, -1), q_rope_swap.reshape(Q_LORA, -1)], axis=1)
    return {
        'w_qk': col(3, 5).astype(BF16),
        'w_v': col(5, 6).astype(BF16),
        'w_gates': col(6, 9).astype(BF16),
        'w_lat': w_lat.astype(BF16),
        'w_q': w_q.astype(BF16),
        'w_uk_t': jnp.swapaxes(w_uk, 1, 2).astype(BF16),
        'w_uv': w_uv.astype(BF16),
        'w_out': w_out.astype(BF16),
        'w_gate': w_gate.astype(BF16),
        'w_up': w_up.astype(BF16),
        'w_down': w_down.astype(BF16),
    }


def _row_tile(m, cap):
    t = min(m, cap)
    assert m % t == 0
    return t


def _layer(x, pos, attend, s0, batch, chunk, narrow_dtype, w, norms):
    m = x.shape[0]
    t_len = m // batch
    tm = _row_tile(m, 1024)
    tm_small = _row_tile(m, 512)

    cos_r, sin_r = _rope_tables(pos, RET_DK // 2)
    cos_r, sin_r = _tile_rows(cos_r, tm), _tile_rows(sin_r, tm)
    cos_m, sin_m = _rope_tables(pos, QK_ROPE // 2)
    reps = LANES // (QK_ROPE // 2)
    cos_m = _tile_rows(jnp.tile(cos_m, (1, reps)), tm_small)
    sin_m = _tile_rows(jnp.tile(jnp.concatenate([-sin_m, sin_m], axis=1), (1, reps // 2)), tm_small)

    h = rmsnorm_rows(x, norms['norm_mix_pre'], tm_small)
    qk = matmul_rope(h, w['w_qk'], cos_r, sin_r, narrow_dtype, tm, 512)
    v = matmul(h, w['w_v'], narrow_dtype, tm, 512)
    gates = matmul(h, w['w_gates'], F32, tm, 512)
    cq, ckv, kpe, kcat = latent_proj(h, w['w_lat'], norms['q_norm'], norms['kv_norm'],
                                     cos_m, sin_m, tm_small)
    q_hm = query_proj(cq, w['w_q'], w['w_uk_t'], cos_m, sin_m, narrow_dtype, tm_small)
    o_lat = attend(q_hm, kcat, ckv, kpe)
    o_mla = value_up_proj(o_lat, w['w_uv'], tm_small)
    mixed, s_new = retention_mix(qk, v, gates, o_mla, norms['ret_norm'], s0,
                                 batch, t_len // chunk, chunk, narrow_dtype)
    x1, f = out_proj(mixed, w['w_out'], x, norms['norm_mix_post'], norms['norm_ffn_pre'],
                     _row_tile(m, 256))
    act = ffn_up(f, w['w_gate'], w['w_up'], tm, 512)
    y = ffn_down(act, w['w_down'], x1, norms['norm_ffn_post'], tm_small, 512)
    return y, ckv, kpe, s_new


def kernel(x_prompt, x_sample, cache_kv_latent, cache_k_rope, page_table, state_retention,
           w_in, q_norm, kv_norm, w_uq, w_uk, w_uv, ret_norm, w_out,
           norm_mix_pre, norm_mix_post, norm_ffn_pre, norm_ffn_post, w_gate, w_up, w_down):
    b, s_len, d = x_prompt.shape
    db, t_s, _ = x_sample.shape
    depth = w_in.shape[0]
    past_len = page_table.shape[1] * PAGE_SIZE
    pos_p = jnp.arange(s_len, dtype=jnp.int32)
    pos_s = past_len + jnp.arange(t_s, dtype=jnp.int32)
    pages_per_step = math.gcd(page_table.shape[1], 8)

    yp = x_prompt.reshape(b * s_len, d)
    ys = x_sample.reshape(db * t_s, d)
    outs = [[] for _ in range(6)]
    for l in range(depth):
        w = _prep_weights(w_in[l], w_uq[l], w_uk[l], w_uv[l], w_out[l], w_gate[l], w_up[l], w_down[l])
        norms = {'q_norm': q_norm[l], 'kv_norm': kv_norm[l], 'ret_norm': ret_norm[l],
                 'norm_mix_pre': norm_mix_pre[l], 'norm_mix_post': norm_mix_post[l],
                 'norm_ffn_pre': norm_ffn_pre[l], 'norm_ffn_post': norm_ffn_post[l]}

        def attend_p(q_hm, kcat, ckv, kpe):
            return prompt_attention(q_hm, kcat, b, s_len, min(128, s_len), min(512, s_len))

        def attend_s(q_hm, kcat, ckv, kpe, l=l):
            return sample_attention(q_hm, cache_kv_latent[l], cache_k_rope[l], page_table,
                                    ckv, kpe, t_s, pages_per_step)

        yp, c1, k1, s1 = _layer(yp, pos_p, attend_p, None, b, min(RET_CHUNK, s_len), BF16, w, norms)
        ys, c2, k2, s2 = _layer(ys, pos_s, attend_s, state_retention[l], db, t_s, F32, w, norms)
        outs[0].append(c1.reshape(b, s_len, KV_LORA))
        outs[1].append(k1.reshape(b, s_len, QK_ROPE))
        outs[2].append(s1)
        outs[3].append(c2.reshape(db, t_s, KV_LORA))
        outs[4].append(k2.reshape(db, t_s, QK_ROPE))
        outs[5].append(s2)
    return (yp.reshape(b, s_len, d), ys.reshape(db, t_s, d)) + tuple(jnp.stack(o) for o in outs)
```

```python
import functools
import math

import jax
import jax.numpy as jnp
from jax import lax
from jax.experimental import pallas as pl
from jax.experimental.pallas import tpu as pltpu

F32 = jnp.float32
BF16 = jnp.bfloat16

D_MODEL = 2048
PAGE_SIZE = 128
MLA_HEADS = 16
Q_LORA = 512
KV_LORA = 512
QK_NOPE = 128
QK_ROPE = 64
V_HEAD = 128
MLA_WIDTH = MLA_HEADS * V_HEAD
MLA_SCALE = (QK_NOPE + QK_ROPE) ** -0.5
SCORE_SCALE_LOG2 = MLA_SCALE * math.log2(math.e)
K_CAT = KV_LORA + QK_ROPE
RET_HEADS = 8
RET_DK = D_MODEL // RET_HEADS
RET_DV = D_MODEL // RET_HEADS
RET_WIDTH = RET_HEADS * RET_DV
RET_K_SCALE = RET_DK ** -0.5
RET_CHUNK = 128
ROPE_THETA = 10000.0
EPS = 1e-6
NEG_INF = -1e30
IN_SIZES = (Q_LORA, KV_LORA, QK_ROPE, RET_HEADS * RET_DK, RET_HEADS * RET_DK,
            RET_WIDTH, RET_WIDTH, MLA_WIDTH, RET_WIDTH)

LANES = 128
VMEM_LIMIT = 56 * 1024 * 1024


def _params(*sem):
    return pltpu.CompilerParams(dimension_semantics=sem, vmem_limit_bytes=VMEM_LIMIT)


def _sigmoid(x):
    return 1.0 / (1.0 + jnp.exp(-x))


def _dot(a, b):
    return jnp.dot(a, b, preferred_element_type=F32)


def _dot_nt(a, b):
    return lax.dot_general(a, b, (((1,), (1,)), ((), ())), preferred_element_type=F32)


def _dot_tn(a, b):
    return lax.dot_general(a, b, (((0,), (0,)), ((), ())), preferred_element_type=F32)


def _rmsnorm_kernel(x_ref, g_ref, o_ref):
    x = x_ref[...]
    ms = jnp.mean(x * x, axis=-1, keepdims=True)
    o_ref[...] = (x * lax.rsqrt(ms + EPS) * g_ref[...]).astype(o_ref.dtype)


def rmsnorm_rows(x, g, tm):
    m, d = x.shape
    return pl.pallas_call(
        _rmsnorm_kernel,
        grid=(m // tm,),
        in_specs=[pl.BlockSpec((tm, d), lambda i: (i, 0)),
                  pl.BlockSpec((1, d), lambda i: (0, 0))],
        out_specs=pl.BlockSpec((tm, d), lambda i: (i, 0)),
        out_shape=jax.ShapeDtypeStruct((m, d), BF16),
        compiler_params=_params("parallel"),
        name="rmsnorm_rows",
    )(x, g.reshape(1, d))


def _mm_kernel(a_ref, w_ref, o_ref):
    o_ref[...] = _dot(a_ref[...], w_ref[...]).astype(o_ref.dtype)


def matmul(a, w, out_dtype, tm, tn):
    m, k = a.shape
    n = w.shape[1]
    return pl.pallas_call(
        _mm_kernel,
        grid=(m // tm, n // tn),
        in_specs=[pl.BlockSpec((tm, k), lambda i, j: (i, 0)),
                  pl.BlockSpec((k, tn), lambda i, j: (0, j))],
        out_specs=pl.BlockSpec((tm, tn), lambda i, j: (i, j)),
        out_shape=jax.ShapeDtypeStruct((m, n), out_dtype),
        compiler_params=_params("parallel", "parallel"),
        name="matmul",
    )(a, w)


def _mm_rope_kernel(a_ref, w_ref, cos_ref, sin_ref, o_ref, *, n_q_tiles, heads_per_tile):
    acc = _dot(a_ref[...], w_ref[...])
    c = cos_ref[...]
    s = sin_ref[...]
    scale = jnp.where(pl.program_id(1) >= n_q_tiles, RET_K_SCALE, 1.0).astype(F32)
    half = RET_DK // 2
    for h in range(heads_per_tile):
        lo = h * RET_DK
        x1 = acc[:, lo:lo + half]
        x2 = acc[:, lo + half:lo + RET_DK]
        o_ref[:, lo:lo + half] = ((x1 * c - x2 * s) * scale).astype(o_ref.dtype)
        o_ref[:, lo + half:lo + RET_DK] = ((x1 * s + x2 * c) * scale).astype(o_ref.dtype)


def matmul_rope(a, w, cos, sin, out_dtype, tm, tn):
    m, k = a.shape
    n = w.shape[1]
    pos_blocks = cos.shape[0] // tm
    kern = functools.partial(_mm_rope_kernel, n_q_tiles=(n // 2) // tn,
                             heads_per_tile=tn // RET_DK)
    return pl.pallas_call(
        kern,
        grid=(m // tm, n // tn),
        in_specs=[pl.BlockSpec((tm, k), lambda i, j: (i, 0)),
                  pl.BlockSpec((k, tn), lambda i, j: (0, j)),
                  pl.BlockSpec((tm, RET_DK // 2), lambda i, j: (i % pos_blocks, 0)),
                  pl.BlockSpec((tm, RET_DK // 2), lambda i, j: (i % pos_blocks, 0))],
        out_specs=pl.BlockSpec((tm, tn), lambda i, j: (i, j)),
        out_shape=jax.ShapeDtypeStruct((m, n), out_dtype),
        compiler_params=_params("parallel", "parallel"),
        name="matmul_rope",
    )(a, w, cos, sin)


def _latent_kernel(a_ref, w_ref, qn_ref, kvn_ref, cos_ref, sin_ref,
                   cq_ref, ckv_ref, kpe_ref, kcat_ref):
    acc = _dot(a_ref[...], w_ref[...])
    cq = acc[:, :Q_LORA]
    cq = cq * lax.rsqrt(jnp.mean(cq * cq, axis=-1, keepdims=True) + EPS) * qn_ref[...]
    cq_ref[...] = cq.astype(cq_ref.dtype)
    ckv = acc[:, Q_LORA:Q_LORA + KV_LORA]
    ckv = ckv * lax.rsqrt(jnp.mean(ckv * ckv, axis=-1, keepdims=True) + EPS) * kvn_ref[...]
    ckv_ref[...] = ckv
    base = Q_LORA + KV_LORA
    x = acc[:, base:base + LANES]
    xs = acc[:, base + LANES:base + 2 * LANES]
    kpe = (x * cos_ref[...] + xs * sin_ref[...])[:, :QK_ROPE]
    kpe_ref[...] = kpe
    kcat_ref[:, :KV_LORA] = ckv.astype(kcat_ref.dtype)
    kcat_ref[:, KV_LORA:K_CAT] = kpe.astype(kcat_ref.dtype)


def latent_proj(a, w, q_norm, kv_norm, cos, sin, tm):
    m, k = a.shape
    n = w.shape[1]
    pos_blocks = cos.shape[0] // tm
    row = lambda i: (i, 0)
    return pl.pallas_call(
        _latent_kernel,
        grid=(m // tm,),
        in_specs=[pl.BlockSpec((tm, k), row),
                  pl.BlockSpec((k, n), lambda i: (0, 0)),
                  pl.BlockSpec((1, Q_LORA), lambda i: (0, 0)),
                  pl.BlockSpec((1, KV_LORA), lambda i: (0, 0)),
                  pl.BlockSpec((tm, LANES), lambda i: (i % pos_blocks, 0)),
                  pl.BlockSpec((tm, LANES), lambda i: (i % pos_blocks, 0))],
        out_specs=[pl.BlockSpec((tm, Q_LORA), row),
                   pl.BlockSpec((tm, KV_LORA), row),
                   pl.BlockSpec((tm, QK_ROPE), row),
                   pl.BlockSpec((tm, K_CAT), row)],
        out_shape=[jax.ShapeDtypeStruct((m, Q_LORA), BF16),
                   jax.ShapeDtypeStruct((m, KV_LORA), F32),
                   jax.ShapeDtypeStruct((m, QK_ROPE), F32),
                   jax.ShapeDtypeStruct((m, K_CAT), BF16)],
        compiler_params=_params("parallel"),
        name="latent_proj",
    )(a, w, q_norm.reshape(1, Q_LORA), kv_norm.reshape(1, KV_LORA), cos, sin)


def _query_kernel(cq_ref, wq_ref, wuk_ref, cos_ref, sin_ref, o_ref):
    q = _dot(cq_ref[...], wq_ref[...])
    for h in range(MLA_HEADS):
        qn = q[:, h * QK_NOPE:(h + 1) * QK_NOPE].astype(BF16)
        o_ref[h, :, :KV_LORA] = (_dot(qn, wuk_ref[h]) * SCORE_SCALE_LOG2).astype(o_ref.dtype)
    c = cos_ref[...] * SCORE_SCALE_LOG2
    s = sin_ref[...] * SCORE_SCALE_LOG2
    rope0 = MLA_HEADS * QK_NOPE
    swap0 = rope0 + MLA_HEADS * QK_ROPE
    for p in range(MLA_HEADS * QK_ROPE // LANES):
        x = q[:, rope0 + p * LANES:rope0 + (p + 1) * LANES]
        xs = q[:, swap0 + p * LANES:swap0 + (p + 1) * LANES]
        qp = x * c + xs * s
        o_ref[2 * p, :, KV_LORA:K_CAT] = qp[:, :QK_ROPE].astype(o_ref.dtype)
        o_ref[2 * p + 1, :, KV_LORA:K_CAT] = qp[:, QK_ROPE:].astype(o_ref.dtype)


def query_proj(cq, wq, wuk_t, cos, sin, out_dtype, tm):
    m = cq.shape[0]
    pos_blocks = cos.shape[0] // tm
    return pl.pallas_call(
        _query_kernel,
        grid=(m // tm,),
        in_specs=[pl.BlockSpec((tm, Q_LORA), lambda i: (i, 0)),
                  pl.BlockSpec(wq.shape, lambda i: (0, 0)),
                  pl.BlockSpec(wuk_t.shape, lambda i: (0, 0, 0)),
                  pl.BlockSpec((tm, LANES), lambda i: (i % pos_blocks, 0)),
                  pl.BlockSpec((tm, LANES), lambda i: (i % pos_blocks, 0))],
        out_specs=pl.BlockSpec((MLA_HEADS, tm, K_CAT), lambda i: (0, i, 0)),
        out_shape=jax.ShapeDtypeStruct((MLA_HEADS, m, K_CAT), out_dtype),
        compiler_params=_params("parallel"),
        name="query_proj",
    )(cq, wq, wuk_t, cos, sin)


def _lane_partial_sum(p):
    out = p[:, :LANES]
    for c in range(1, p.shape[1] // LANES):
        out = out + p[:, c * LANES:(c + 1) * LANES]
    return out


def _softmax_update(s, v, m_prev, l_prev, acc_prev):
    m_new = jnp.maximum(m_prev, jnp.max(s, axis=-1, keepdims=True))
    corr = jnp.exp2(m_prev - m_new)
    p = jnp.exp2(s - m_new)
    l_new = l_prev * corr + _lane_partial_sum(p)
    acc_new = acc_prev * corr + _dot(p.astype(BF16), v)
    return m_new, l_new, acc_new


def _softmax_finish(l, acc):
    return acc / jnp.sum(l, axis=-1, keepdims=True)


PAIR_FIRST, PAIR_LAST, PAIR_MASKED = 1, 2, 4
ATTN_HEAD_GROUPS = 8


def _prompt_attn_kernel(it_ref, jt_ref, fl_ref, q_ref, k_ref, o_ref, m_ref, l_ref, acc_ref,
                        *, tq, tk):
    p = pl.program_id(1)
    i = it_ref[p]
    j = jt_ref[p]
    flags = fl_ref[p]
    hg = MLA_HEADS // ATTN_HEAD_GROUPS
    rows = hg * tq

    @pl.when((flags & PAIR_FIRST) != 0)
    def _():
        m_ref[...] = jnp.full(m_ref.shape, NEG_INF, F32)
        l_ref[...] = jnp.zeros(l_ref.shape, F32)
        acc_ref[...] = jnp.zeros(acc_ref.shape, F32)

    def step(masked):
        k = k_ref[...]
        v = k[:, :KV_LORA]
        if masked:
            qpos = i * tq + lax.broadcasted_iota(jnp.int32, (tq, tk), 0)
            kpos = j * tk + lax.broadcasted_iota(jnp.int32, (tq, tk), 1)
            keep = (kpos <= qpos)[None]
        def scores(g):
            q = q_ref[g * hg:(g + 1) * hg].reshape(rows, K_CAT)
            s = _dot_nt(q, k)
            if masked:
                s = jnp.where(keep, s.reshape(hg, tq, tk), NEG_INF).reshape(rows, tk)
            return s

        s_next = scores(0)
        for g in range(ATTN_HEAD_GROUPS):
            rs = slice(g * rows, (g + 1) * rows)
            s = s_next
            if g + 1 < ATTN_HEAD_GROUPS:
                s_next = scores(g + 1)
            m_new, l_new, acc_new = _softmax_update(s, v, m_ref[rs], l_ref[rs], acc_ref[rs])
            m_ref[rs] = m_new
            l_ref[rs] = l_new
            acc_ref[rs] = acc_new

    @pl.when((flags & PAIR_MASKED) == 0)
    def _():
        step(False)

    @pl.when((flags & PAIR_MASKED) != 0)
    def _():
        step(True)

    @pl.when((flags & PAIR_LAST) != 0)
    def _():
        o = _softmax_finish(l_ref[...], acc_ref[...])
        o_ref[...] = o.reshape(MLA_HEADS, tq, KV_LORA).astype(o_ref.dtype)


def prompt_attention(q_hm, kcat, batch, seq, tq, tk):
    nq = seq // tq
    nk = seq // tk
    rows = MLA_HEADS * tq
    it, jt, fl = [], [], []
    for i in range(nq):
        last_j = (i * tq + tq - 1) // tk
        for j in range(last_j + 1):
            masked = j * tk + tk - 1 > i * tq
            it.append(i)
            jt.append(j)
            fl.append((PAIR_FIRST if j == 0 else 0) | (PAIR_LAST if j == last_j else 0)
                      | (PAIR_MASKED if masked else 0))
    tables = [jnp.asarray(t, jnp.int32) for t in (it, jt, fl)]
    q_map = lambda b, p, it, jt, fl: (0, b * nq + it[p], 0)
    kern = functools.partial(_prompt_attn_kernel, tq=tq, tk=tk)
    grid_spec = pltpu.PrefetchScalarGridSpec(
        num_scalar_prefetch=3,
        grid=(batch, len(it)),
        in_specs=[pl.BlockSpec((MLA_HEADS, tq, K_CAT), q_map),
                  pl.BlockSpec((tk, K_CAT), lambda b, p, it, jt, fl: (b * nk + jt[p], 0))],
        out_specs=pl.BlockSpec((MLA_HEADS, tq, KV_LORA), q_map),
        scratch_shapes=[pltpu.VMEM((rows, 1), F32),
                        pltpu.VMEM((rows, LANES), F32),
                        pltpu.VMEM((rows, KV_LORA), F32)],
    )
    return pl.pallas_call(
        kern,
        grid_spec=grid_spec,
        out_shape=jax.ShapeDtypeStruct((MLA_HEADS, batch * seq, KV_LORA), BF16),
        compiler_params=_params("parallel", "arbitrary"),
        name="prompt_attention",
    )(*tables, q_hm, kcat)


def _sample_attn_kernel(pt_ref, q_ref, *refs, group, pages_per_step, t_new):
    del pt_ref
    n_pg = group * pages_per_step
    ckv_refs = refs[:n_pg]
    kpe_refs = refs[n_pg:2 * n_pg]
    (ckv_new_ref, kpe_new_ref, o_ref,
     cbuf_ref, rbuf_ref, knew_ref, m_ref, l_ref, acc_ref) = refs[2 * n_pg:]
    j = pl.program_id(1)
    rows = MLA_HEADS * t_new

    @pl.when(j == 0)
    def _():
        m_ref[...] = jnp.full(m_ref.shape, NEG_INF, F32)
        l_ref[...] = jnp.zeros(l_ref.shape, F32)
        acc_ref[...] = jnp.zeros(acc_ref.shape, F32)

    def query(g):
        return q_ref[:, g * t_new:(g + 1) * t_new, :].reshape(rows, K_CAT).astype(BF16)

    def scores(g):
        for u in range(pages_per_step):
            ps = slice(u * PAGE_SIZE, (u + 1) * PAGE_SIZE)
            cbuf_ref[g, ps, :] = ckv_refs[g * pages_per_step + u][...].astype(BF16)
            rbuf_ref[g, :, ps] = kpe_refs[g * pages_per_step + u][...].astype(BF16)
        q = query(g)
        return _dot_nt(q[:, :KV_LORA], cbuf_ref[g]) + _dot(q[:, KV_LORA:], rbuf_ref[g])

    s_next = scores(0)
    for g in range(group):
        s = s_next
        if g + 1 < group:
            s_next = scores(g + 1)
        m_new, l_new, acc_new = _softmax_update(s, cbuf_ref[g], m_ref[g], l_ref[g], acc_ref[g])
        m_ref[g] = m_new
        l_ref[g] = l_new
        acc_ref[g] = acc_new

    @pl.when(j == pl.num_programs(1) - 1)
    def _():
        qpos = lax.broadcasted_iota(jnp.int32, (t_new, PAGE_SIZE), 0)
        kpos = lax.broadcasted_iota(jnp.int32, (t_new, PAGE_SIZE), 1)
        keep = (kpos <= qpos)[None]
        for g in range(group):
            ts = slice(g * t_new, (g + 1) * t_new)
            knew_ref[...] = jnp.zeros(knew_ref.shape, F32)
            knew_ref[:t_new, :KV_LORA] = ckv_new_ref[ts, :]
            knew_ref[:t_new, KV_LORA:K_CAT] = kpe_new_ref[ts, :]
            kn = knew_ref[...].astype(BF16)
            sn = _dot_nt(query(g), kn)
            sn = jnp.where(keep, sn.reshape(MLA_HEADS, t_new, PAGE_SIZE), NEG_INF)
            _, l_new, acc_new = _softmax_update(sn.reshape(rows, PAGE_SIZE), kn[:, :KV_LORA],
                                                m_ref[g], l_ref[g], acc_ref[g])
            o = _softmax_finish(l_new, acc_new)
            o_ref[:, ts, :] = o.reshape(MLA_HEADS, t_new, KV_LORA).astype(o_ref.dtype)


def sample_attention(q_hm, cache_ckv, cache_kpe_t, page_table, ckv_new, kpe_new, t_new,
                     group, pages_per_step):
    dec_batch, n_pages = page_table.shape
    n_steps = n_pages // pages_per_step
    rows = MLA_HEADS * t_new
    gt = group * t_new

    def page_spec(shape, g, u):
        return pl.BlockSpec((None,) + shape,
                            lambda b, j, pt: (pt[b * group + g, j * pages_per_step + u], 0, 0))

    slots = [(g, u) for g in range(group) for u in range(pages_per_step)]
    in_specs = [pl.BlockSpec((MLA_HEADS, gt, K_CAT), lambda b, j, pt: (0, b, 0))]
    in_specs += [page_spec((PAGE_SIZE, KV_LORA), g, u) for g, u in slots]
    in_specs += [page_spec((QK_ROPE, PAGE_SIZE), g, u) for g, u in slots]
    in_specs += [pl.BlockSpec((gt, KV_LORA), lambda b, j, pt: (b, 0)),
                 pl.BlockSpec((gt, QK_ROPE), lambda b, j, pt: (b, 0))]
    kern = functools.partial(_sample_attn_kernel, group=group,
                             pages_per_step=pages_per_step, t_new=t_new)
    keys = pages_per_step * PAGE_SIZE
    grid_spec = pltpu.PrefetchScalarGridSpec(
        num_scalar_prefetch=1,
        grid=(dec_batch // group, n_steps),
        in_specs=in_specs,
        out_specs=pl.BlockSpec((MLA_HEADS, gt, KV_LORA), lambda b, j, pt: (0, b, 0)),
        scratch_shapes=[pltpu.VMEM((group, keys, KV_LORA), BF16),
                        pltpu.VMEM((group, QK_ROPE, keys), BF16),
                        pltpu.VMEM((PAGE_SIZE, K_CAT), F32),
                        pltpu.VMEM((group, rows, 1), F32),
                        pltpu.VMEM((group, rows, LANES), F32),
                        pltpu.VMEM((group, rows, KV_LORA), F32)],
    )
    args = [page_table, q_hm] + [cache_ckv] * len(slots) + [cache_kpe_t] * len(slots)
    args += [ckv_new, kpe_new]
    return pl.pallas_call(
        kern,
        grid_spec=grid_spec,
        out_shape=jax.ShapeDtypeStruct((MLA_HEADS, dec_batch * t_new, KV_LORA), F32),
        compiler_params=_params("parallel", "arbitrary"),
        name="sample_attention",
    )(*args)


def _uv_kernel(o_ref, w_ref, out_ref):
    for h in range(MLA_HEADS):
        out_ref[:, h * V_HEAD:(h + 1) * V_HEAD] = _dot(o_ref[h].astype(BF16), w_ref[h])


def value_up_proj(o_lat, w_uv, tm):
    heads, m, _ = o_lat.shape
    return pl.pallas_call(
        _uv_kernel,
        grid=(m // tm,),
        in_specs=[pl.BlockSpec((heads, tm, KV_LORA), lambda i: (0, i, 0)),
                  pl.BlockSpec((heads, KV_LORA, V_HEAD), lambda i: (0, 0, 0))],
        out_specs=pl.BlockSpec((tm, heads * V_HEAD), lambda i: (i, 0)),
        out_shape=jax.ShapeDtypeStruct((m, heads * V_HEAD), F32),
        compiler_params=_params("parallel"),
        name="value_up_proj",
    )(o_lat, w_uv)


def _retention_kernel(*refs, chunk, has_s0):
    if has_s0:
        (q_ref, k_ref, v_ref, rg_ref, gm_ref, gr_ref, om_ref, rn_ref, s0_ref,
         mix_ref, sout_ref, s_ref) = refs
    else:
        (q_ref, k_ref, v_ref, rg_ref, gm_ref, gr_ref, om_ref, rn_ref,
         mix_ref, sout_ref, s_ref) = refs
    c = pl.program_id(1)

    @pl.when(c == 0)
    def _():
        if has_s0:
            s_ref[...] = s0_ref[0]
        else:
            s_ref[...] = jnp.zeros(s_ref.shape, F32)

    ii = lax.broadcasted_iota(jnp.int32, (chunk, chunk), 0)
    jj = lax.broadcasted_iota(jnp.int32, (chunk, chunk), 1)
    diff = (ii - jj).astype(F32)
    idx = lax.broadcasted_iota(jnp.int32, (chunk, 1), 0).astype(F32)
    for h in range(RET_HEADS):
        lg = math.log(1.0 - 2.0 ** (-5.0 - h))
        dmat = jnp.where(diff >= 0, jnp.exp(jnp.maximum(diff, 0.0) * lg), 0.0)
        q_dec = jnp.exp((idx + 1.0) * lg)
        k_dec = jnp.exp((chunk - 1.0 - idx) * lg)
        c_dec = math.exp(chunk * lg)
        sl = slice(h * RET_DK, (h + 1) * RET_DK)
        q = q_ref[:, sl].astype(BF16)
        k = k_ref[:, sl]
        v = v_ref[:, sl].astype(BF16)
        state = s_ref[h]
        a = _dot_nt(q, k.astype(BF16)) * dmat
        o = _dot(a.astype(BF16), v) + _dot(q, state.astype(BF16)) * q_dec
        kd = (k.astype(F32) * k_dec).astype(BF16)
        s_ref[h] = state * c_dec + _dot_tn(kd, v)
        mu = jnp.mean(o, axis=-1, keepdims=True)
        d = o - mu
        var = jnp.mean(d * d, axis=-1, keepdims=True)
        on = d * lax.rsqrt(var + EPS) * rn_ref[:, sl]
        rg = rg_ref[:, sl]
        o_ret = rg * _sigmoid(rg) * on
        mixed = _sigmoid(gm_ref[:, sl]) * om_ref[:, sl] + _sigmoid(gr_ref[:, sl]) * o_ret
        mix_ref[:, sl] = mixed.astype(mix_ref.dtype)

    @pl.when(c == pl.num_programs(1) - 1)
    def _():
        sout_ref[0] = s_ref[...]


def retention_mix(qk, v, gates, o_mla, ret_norm, s0, batch, n_chunks, chunk, mix_dtype):
    m = v.shape[0]
    has_s0 = s0 is not None
    row = lambda col: (lambda b, c: (b * n_chunks + c, col))
    blk = lambda col: pl.BlockSpec((chunk, RET_WIDTH), row(col))
    state_spec = pl.BlockSpec((1, RET_HEADS, RET_DK, RET_DV), lambda b, c: (b, 0, 0, 0))
    in_specs = [blk(0), blk(1), blk(0), blk(0), blk(1), blk(2), blk(0),
                pl.BlockSpec((1, RET_WIDTH), lambda b, c: (0, 0))]
    args = [qk, qk, v, gates, gates, gates, o_mla, ret_norm.reshape(1, RET_WIDTH)]
    if has_s0:
        in_specs.append(state_spec)
        args.append(s0)
    kern = functools.partial(_retention_kernel, chunk=chunk, has_s0=has_s0)
    return pl.pallas_call(
        kern,
        grid=(batch, n_chunks),
        in_specs=in_specs,
        out_specs=[blk(0), state_spec],
        out_shape=[jax.ShapeDtypeStruct((m, RET_WIDTH), mix_dtype),
                   jax.ShapeDtypeStruct((batch, RET_HEADS, RET_DK, RET_DV), F32)],
        scratch_shapes=[pltpu.VMEM((RET_HEADS, RET_DK, RET_DV), F32)],
        compiler_params=_params("parallel", "arbitrary"),
        name="retention_mix",
    )(*args)


def _out_proj_kernel(mix_ref, w_ref, x_ref, gpost_ref, gpre_ref, x1_ref, f_ref):
    y = _dot(mix_ref[...].astype(BF16), w_ref[...])
    y = y * lax.rsqrt(jnp.mean(y * y, axis=-1, keepdims=True) + EPS) * gpost_ref[...]
    x1 = x_ref[...] + y
    x1_ref[...] = x1
    f = x1 * lax.rsqrt(jnp.mean(x1 * x1, axis=-1, keepdims=True) + EPS) * gpre_ref[...]
    f_ref[...] = f.astype(f_ref.dtype)


def out_proj(mixed, w_out, x, g_post, g_pre, tm):
    m, d = x.shape
    row = lambda i: (i, 0)
    vec = pl.BlockSpec((1, d), lambda i: (0, 0))
    return pl.pallas_call(
        _out_proj_kernel,
        grid=(m // tm,),
        in_specs=[pl.BlockSpec((tm, d), row), pl.BlockSpec((d, d), lambda i: (0, 0)),
                  pl.BlockSpec((tm, d), row), vec, vec],
        out_specs=[pl.BlockSpec((tm, d), row), pl.BlockSpec((tm, d), row)],
        out_shape=[jax.ShapeDtypeStruct((m, d), F32), jax.ShapeDtypeStruct((m, d), BF16)],
        compiler_params=_params("parallel"),
        name="out_proj",
    )(mixed, w_out, x, g_post.reshape(1, d), g_pre.reshape(1, d))


def _ffn_up_kernel(f_ref, wg_ref, wu_ref, o_ref):
    f = f_ref[...]
    g = _dot(f, wg_ref[...])
    u = _dot(f, wu_ref[...])
    o_ref[...] = (g * _sigmoid(g) * u).astype(o_ref.dtype)


def ffn_up(f, w_gate, w_up, tm, tn):
    m, d = f.shape
    n = w_gate.shape[1]
    return pl.pallas_call(
        _ffn_up_kernel,
        grid=(m // tm, n // tn),
        in_specs=[pl.BlockSpec((tm, d), lambda i, j: (i, 0)),
                  pl.BlockSpec((d, tn), lambda i, j: (0, j)),
                  pl.BlockSpec((d, tn), lambda i, j: (0, j))],
        out_specs=pl.BlockSpec((tm, tn), lambda i, j: (i, j)),
        out_shape=jax.ShapeDtypeStruct((m, n), BF16),
        compiler_params=_params("parallel", "parallel"),
        name="ffn_up",
    )(f, w_gate, w_up)


def _ffn_down_kernel(a_ref, w_ref, x_ref, g_ref, o_ref):
    kk = pl.program_id(1)

    @pl.when(kk == 0)
    def _():
        o_ref[...] = jnp.zeros(o_ref.shape, F32)

    o_ref[...] += _dot(a_ref[...], w_ref[...])

    @pl.when(kk == pl.num_programs(1) - 1)
    def _():
        y = o_ref[...]
        y = y * lax.rsqrt(jnp.mean(y * y, axis=-1, keepdims=True) + EPS) * g_ref[...]
        o_ref[...] = x_ref[...] + y


def ffn_down(act, w_down, x1, g_post, tm, tk):
    m, dff = act.shape
    d = w_down.shape[1]
    return pl.pallas_call(
        _ffn_down_kernel,
        grid=(m // tm, dff // tk),
        in_specs=[pl.BlockSpec((tm, tk), lambda i, k: (i, k)),
                  pl.BlockSpec((tk, d), lambda i, k: (k, 0)),
                  pl.BlockSpec((tm, d), lambda i, k: (i, 0)),
                  pl.BlockSpec((1, d), lambda i, k: (0, 0))],
        out_specs=pl.BlockSpec((tm, d), lambda i, k: (i, 0)),
        out_shape=jax.ShapeDtypeStruct((m, d), F32),
        compiler_params=_params("parallel", "arbitrary"),
        name="ffn_down",
    )(act, w_down, x1, g_post.reshape(1, d))


def _rope_tables(pos, half):
    inv = ROPE_THETA ** (-jnp.arange(half, dtype=F32) / half)
    ang = pos.astype(F32)[:, None] * inv[None, :]
    return jnp.cos(ang), jnp.sin(ang)


def _tile_rows(table, rows):
    reps = max(1, rows // table.shape[0])
    return jnp.tile(table, (reps, 1))


def _prep_weights(w_in, w_uq, w_uk, w_uv, w_out, w_gate, w_up, w_down):
    offs = [0]
    for sz in IN_SIZES:
        offs.append(offs[-1] + sz)
    col = lambda a, b: w_in[:, offs[a]:offs[b]]
    k_pe = col(2, 3)
    half = QK_ROPE // 2
    k_pe_swap = jnp.concatenate([k_pe[:, half:], k_pe[:, :half]], axis=1)
    zpad = jnp.zeros((w_in.shape[0], LANES - QK_ROPE), w_in.dtype)
    w_lat = jnp.concatenate([col(0, 2), k_pe, zpad, k_pe_swap, zpad], axis=1)
    q_nope = w_uq[:, :, :QK_NOPE].reshape(Q_LORA, MLA_HEADS * QK_NOPE)
    q_rope = w_uq[:, :, QK_NOPE:]
    q_rope_swap = jnp.concatenate([q_rope[:, :, half:], q_rope[:, :, :half]], axis=2)
    w_q = jnp.concatenate([q_nope, q_rope.reshape(Q_LORA, -1), q_rope_swap.reshape(Q_LORA, -1)], axis=1)
    return {
        'w_qk': col(3, 5).astype(BF16),
        'w_v': col(5, 6).astype(BF16),
        'w_gates': col(6, 9).astype(BF16),
        'w_lat': w_lat.astype(BF16),
        'w_q': w_q.astype(BF16),
        'w_uk_t': jnp.swapaxes(w_uk, 1, 2).astype(BF16),
        'w_uv': w_uv.astype(BF16),
        'w_out': w_out.astype(BF16),
        'w_gate': w_gate.astype(BF16),
        'w_up': w_up.astype(BF16),
        'w_down': w_down.astype(BF16),
    }


def _row_tile(m, cap):
    t = min(m, cap)
    assert m % t == 0
    return t


def _layer(x, pos, attend, s0, batch, chunk, narrow_dtype, w, norms):
    m = x.shape[0]
    t_len = m // batch
    tm = _row_tile(m, 1024)
    tm_small = _row_tile(m, 512)

    cos_r, sin_r = _rope_tables(pos, RET_DK // 2)
    cos_r, sin_r = _tile_rows(cos_r, tm), _tile_rows(sin_r, tm)
    cos_m, sin_m = _rope_tables(pos, QK_ROPE // 2)
    reps = LANES // (QK_ROPE // 2)
    cos_m = _tile_rows(jnp.tile(cos_m, (1, reps)), tm_small)
    sin_m = _tile_rows(jnp.tile(jnp.concatenate([-sin_m, sin_m], axis=1), (1, reps // 2)), tm_small)

    h = rmsnorm_rows(x, norms['norm_mix_pre'], tm_small)
    qk = matmul_rope(h, w['w_qk'], cos_r, sin_r, narrow_dtype, tm, 512)
    v = matmul(h, w['w_v'], narrow_dtype, tm, 512)
    gates = matmul(h, w['w_gates'], F32, tm, 512)
    cq, ckv, kpe, kcat = latent_proj(h, w['w_lat'], norms['q_norm'], norms['kv_norm'],
                                     cos_m, sin_m, tm_small)
    q_hm = query_proj(cq, w['w_q'], w['w_uk_t'], cos_m, sin_m, narrow_dtype, tm_small)
    o_lat = attend(q_hm, kcat, ckv, kpe)
    o_mla = value_up_proj(o_lat, w['w_uv'], _row_tile(m, 256))
    mixed, s_new = retention_mix(qk, v, gates, o_mla, norms['ret_norm'], s0,
                                 batch, t_len // chunk, chunk, narrow_dtype)
    x1, f = out_proj(mixed, w['w_out'], x, norms['norm_mix_post'], norms['norm_ffn_pre'],
                     _row_tile(m, 256))
    act = ffn_up(f, w['w_gate'], w['w_up'], tm, 512)
    y = ffn_down(act, w['w_down'], x1, norms['norm_ffn_post'], tm, 512)
    return y, ckv, kpe, s_new


def kernel(x_prompt, x_sample, cache_kv_latent, cache_k_rope, page_table, state_retention,
           w_in, q_norm, kv_norm, w_uq, w_uk, w_uv, ret_norm, w_out,
           norm_mix_pre, norm_mix_post, norm_ffn_pre, norm_ffn_post, w_gate, w_up, w_down):
    b, s_len, d = x_prompt.shape
    db, t_s, _ = x_sample.shape
    depth = w_in.shape[0]
    past_len = page_table.shape[1] * PAGE_SIZE
    pos_p = jnp.arange(s_len, dtype=jnp.int32)
    pos_s = past_len + jnp.arange(t_s, dtype=jnp.int32)
    pages_per_step = math.gcd(page_table.shape[1], 8)
    group = math.gcd(db, 4)

    yp = x_prompt.reshape(b * s_len, d)
    ys = x_sample.reshape(db * t_s, d)
    outs = [[] for _ in range(6)]
    for l in range(depth):
        w = _prep_weights(w_in[l], w_uq[l], w_uk[l], w_uv[l], w_out[l], w_gate[l], w_up[l], w_down[l])
        norms = {'q_norm': q_norm[l], 'kv_norm': kv_norm[l], 'ret_norm': ret_norm[l],
                 'norm_mix_pre': norm_mix_pre[l], 'norm_mix_post': norm_mix_post[l],
                 'norm_ffn_pre': norm_ffn_pre[l], 'norm_ffn_post': norm_ffn_post[l]}

        def attend_p(q_hm, kcat, ckv, kpe):
            return prompt_attention(q_hm, kcat, b, s_len, min(128, s_len), min(512, s_len))

        def attend_s(q_hm, kcat, ckv, kpe, l=l):
            cache_kpe_t = jnp.swapaxes(cache_k_rope[l], 1, 2)
            return sample_attention(q_hm, cache_kv_latent[l], cache_kpe_t, page_table,
                                    ckv, kpe, t_s, group, pages_per_step)

        yp, c1, k1, s1 = _layer(yp, pos_p, attend_p, None, b, min(RET_CHUNK, s_len), BF16, w, norms)
        ys, c2, k2, s2 = _layer(ys, pos_s, attend_s, state_retention[l], db, t_s, F32, w, norms)
        outs[0].append(c1.reshape(b, s_len, KV_LORA))
        outs[1].append(k1.reshape(b, s_len, QK_ROPE))
        outs[2].append(s1)
        outs[3].append(c2.reshape(db, t_s, KV_LORA))
        outs[4].append(k2.reshape(db, t_s, QK_ROPE))
        outs[5].append(s2)
    return (yp.reshape(b, s_len, d), ys.reshape(db, t_s, d)) + tuple(jnp.stack(o) for o in outs)
```

```python
import functools
import math

import jax
import jax.numpy as jnp
from jax import lax
from jax.experimental import pallas as pl
from jax.experimental.pallas import tpu as pltpu

F32 = jnp.float32
BF16 = jnp.bfloat16

D_MODEL = 2048
PAGE_SIZE = 128
MLA_HEADS = 16
Q_LORA = 512
KV_LORA = 512
QK_NOPE = 128
QK_ROPE = 64
V_HEAD = 128
MLA_WIDTH = MLA_HEADS * V_HEAD
MLA_SCALE = (QK_NOPE + QK_ROPE) ** -0.5
SCORE_SCALE_LOG2 = MLA_SCALE * math.log2(math.e)
K_CAT = KV_LORA + QK_ROPE
RET_HEADS = 8
RET_DK = D_MODEL // RET_HEADS
RET_DV = D_MODEL // RET_HEADS
RET_WIDTH = RET_HEADS * RET_DV
RET_K_SCALE = RET_DK ** -0.5
RET_CHUNK = 128
ROPE_THETA = 10000.0
EPS = 1e-6
NEG_INF = -1e30
IN_SIZES = (Q_LORA, KV_LORA, QK_ROPE, RET_HEADS * RET_DK, RET_HEADS * RET_DK,
            RET_WIDTH, RET_WIDTH, MLA_WIDTH, RET_WIDTH)

LANES = 128
VMEM_LIMIT = 56 * 1024 * 1024


def _params(*sem):
    return pltpu.CompilerParams(dimension_semantics=sem, vmem_limit_bytes=VMEM_LIMIT)


def _sigmoid(x):
    return 1.0 / (1.0 + jnp.exp(-x))


def _dot(a, b):
    return jnp.dot(a, b, preferred_element_type=F32)


def _dot_nt(a, b):
    return lax.dot_general(a, b, (((1,), (1,)), ((), ())), preferred_element_type=F32)


def _dot_tn(a, b):
    return lax.dot_general(a, b, (((0,), (0,)), ((), ())), preferred_element_type=F32)


def _rmsnorm_kernel(x_ref, g_ref, o_ref):
    x = x_ref[...]
    ms = jnp.mean(x * x, axis=-1, keepdims=True)
    o_ref[...] = (x * lax.rsqrt(ms + EPS) * g_ref[...]).astype(o_ref.dtype)


def rmsnorm_rows(x, g, tm):
    m, d = x.shape
    return pl.pallas_call(
        _rmsnorm_kernel,
        grid=(m // tm,),
        in_specs=[pl.BlockSpec((tm, d), lambda i: (i, 0)),
                  pl.BlockSpec((1, d), lambda i: (0, 0))],
        out_specs=pl.BlockSpec((tm, d), lambda i: (i, 0)),
        out_shape=jax.ShapeDtypeStruct((m, d), BF16),
        compiler_params=_params("parallel"),
        name="rmsnorm_rows",
    )(x, g.reshape(1, d))


def _mm_kernel(a_ref, w_ref, o_ref):
    o_ref[...] = _dot(a_ref[...], w_ref[...]).astype(o_ref.dtype)


def matmul(a, w, out_dtype, tm, tn):
    m, k = a.shape
    n = w.shape[1]
    return pl.pallas_call(
        _mm_kernel,
        grid=(m // tm, n // tn),
        in_specs=[pl.BlockSpec((tm, k), lambda i, j: (i, 0)),
                  pl.BlockSpec((k, tn), lambda i, j: (0, j))],
        out_specs=pl.BlockSpec((tm, tn), lambda i, j: (i, j)),
        out_shape=jax.ShapeDtypeStruct((m, n), out_dtype),
        compiler_params=_params("parallel", "parallel"),
        name="matmul",
    )(a, w)


def _mm_rope_kernel(a_ref, w_ref, cos_ref, sin_ref, o_ref, *, n_q_tiles, heads_per_tile):
    acc = _dot(a_ref[...], w_ref[...])
    c = cos_ref[...]
    s = sin_ref[...]
    scale = jnp.where(pl.program_id(1) >= n_q_tiles, RET_K_SCALE, 1.0).astype(F32)
    half = RET_DK // 2
    for h in range(heads_per_tile):
        lo = h * RET_DK
        x1 = acc[:, lo:lo + half]
        x2 = acc[:, lo + half:lo + RET_DK]
        o_ref[:, lo:lo + half] = ((x1 * c - x2 * s) * scale).astype(o_ref.dtype)
        o_ref[:, lo + half:lo + RET_DK] = ((x1 * s + x2 * c) * scale).astype(o_ref.dtype)


def matmul_rope(a, w, cos, sin, out_dtype, tm, tn):
    m, k = a.shape
    n = w.shape[1]
    pos_blocks = cos.shape[0] // tm
    kern = functools.partial(_mm_rope_kernel, n_q_tiles=(n // 2) // tn,
                             heads_per_tile=tn // RET_DK)
    return pl.pallas_call(
        kern,
        grid=(m // tm, n // tn),
        in_specs=[pl.BlockSpec((tm, k), lambda i, j: (i, 0)),
                  pl.BlockSpec((k, tn), lambda i, j: (0, j)),
                  pl.BlockSpec((tm, RET_DK // 2), lambda i, j: (i % pos_blocks, 0)),
                  pl.BlockSpec((tm, RET_DK // 2), lambda i, j: (i % pos_blocks, 0))],
        out_specs=pl.BlockSpec((tm, tn), lambda i, j: (i, j)),
        out_shape=jax.ShapeDtypeStruct((m, n), out_dtype),
        compiler_params=_params("parallel", "parallel"),
        name="matmul_rope",
    )(a, w, cos, sin)


def _latent_kernel(a_ref, w_ref, qn_ref, kvn_ref, cos_ref, sin_ref,
                   cq_ref, ckv_ref, kpe_ref, kcat_ref):
    acc = _dot(a_ref[...], w_ref[...])
    cq = acc[:, :Q_LORA]
    cq = cq * lax.rsqrt(jnp.mean(cq * cq, axis=-1, keepdims=True) + EPS) * qn_ref[...]
    cq_ref[...] = cq.astype(cq_ref.dtype)
    ckv = acc[:, Q_LORA:Q_LORA + KV_LORA]
    ckv = ckv * lax.rsqrt(jnp.mean(ckv * ckv, axis=-1, keepdims=True) + EPS) * kvn_ref[...]
    ckv_ref[...] = ckv
    base = Q_LORA + KV_LORA
    x = acc[:, base:base + LANES]
    xs = acc[:, base + LANES:base + 2 * LANES]
    kpe = (x * cos_ref[...] + xs * sin_ref[...])[:, :QK_ROPE]
    kpe_ref[...] = kpe
    kcat_ref[:, :KV_LORA] = ckv.astype(kcat_ref.dtype)
    kcat_ref[:, KV_LORA:K_CAT] = kpe.astype(kcat_ref.dtype)


def latent_proj(a, w, q_norm, kv_norm, cos, sin, tm):
    m, k = a.shape
    n = w.shape[1]
    pos_blocks = cos.shape[0] // tm
    row = lambda i: (i, 0)
    return pl.pallas_call(
        _latent_kernel,
        grid=(m // tm,),
        in_specs=[pl.BlockSpec((tm, k), row),
                  pl.BlockSpec((k, n), lambda i: (0, 0)),
                  pl.BlockSpec((1, Q_LORA), lambda i: (0, 0)),
                  pl.BlockSpec((1, KV_LORA), lambda i: (0, 0)),
                  pl.BlockSpec((tm, LANES), lambda i: (i % pos_blocks, 0)),
                  pl.BlockSpec((tm, LANES), lambda i: (i % pos_blocks, 0))],
        out_specs=[pl.BlockSpec((tm, Q_LORA), row),
                   pl.BlockSpec((tm, KV_LORA), row),
                   pl.BlockSpec((tm, QK_ROPE), row),
                   pl.BlockSpec((tm, K_CAT), row)],
        out_shape=[jax.ShapeDtypeStruct((m, Q_LORA), BF16),
                   jax.ShapeDtypeStruct((m, KV_LORA), F32),
                   jax.ShapeDtypeStruct((m, QK_ROPE), F32),
                   jax.ShapeDtypeStruct((m, K_CAT), BF16)],
        compiler_params=_params("parallel"),
        name="latent_proj",
    )(a, w, q_norm.reshape(1, Q_LORA), kv_norm.reshape(1, KV_LORA), cos, sin)


def _query_kernel(cq_ref, wq_ref, wuk_ref, cos_ref, sin_ref, o_ref):
    q = _dot(cq_ref[...], wq_ref[...])
    for h in range(MLA_HEADS):
        qn = q[:, h * QK_NOPE:(h + 1) * QK_NOPE].astype(BF16)
        o_ref[h, :, :KV_LORA] = (_dot(qn, wuk_ref[h]) * SCORE_SCALE_LOG2).astype(o_ref.dtype)
    c = cos_ref[...] * SCORE_SCALE_LOG2
    s = sin_ref[...] * SCORE_SCALE_LOG2
    rope0 = MLA_HEADS * QK_NOPE
    swap0 = rope0 + MLA_HEADS * QK_ROPE
    for p in range(MLA_HEADS * QK_ROPE // LANES):
        x = q[:, rope0 + p * LANES:rope0 + (p + 1) * LANES]
        xs = q[:, swap0 + p * LANES:swap0 + (p + 1) * LANES]
        qp = x * c + xs * s
        o_ref[2 * p, :, KV_LORA:K_CAT] = qp[:, :QK_ROPE].astype(o_ref.dtype)
        o_ref[2 * p + 1, :, KV_LORA:K_CAT] = qp[:, QK_ROPE:].astype(o_ref.dtype)


def query_proj(cq, wq, wuk_t, cos, sin, out_dtype, tm):
    m = cq.shape[0]
    pos_blocks = cos.shape[0] // tm
    return pl.pallas_call(
        _query_kernel,
        grid=(m // tm,),
        in_specs=[pl.BlockSpec((tm, Q_LORA), lambda i: (i, 0)),
                  pl.BlockSpec(wq.shape, lambda i: (0, 0)),
                  pl.BlockSpec(wuk_t.shape, lambda i: (0, 0, 0)),
                  pl.BlockSpec((tm, LANES), lambda i: (i % pos_blocks, 0)),
                  pl.BlockSpec((tm, LANES), lambda i: (i % pos_blocks, 0))],
        out_specs=pl.BlockSpec((MLA_HEADS, tm, K_CAT), lambda i: (0, i, 0)),
        out_shape=jax.ShapeDtypeStruct((MLA_HEADS, m, K_CAT), out_dtype),
        compiler_params=_params("parallel"),
        name="query_proj",
    )(cq, wq, wuk_t, cos, sin)


def _lane_partial_sum(p):
    out = p[:, :LANES]
    for c in range(1, p.shape[1] // LANES):
        out = out + p[:, c * LANES:(c + 1) * LANES]
    return out


def _softmax_update(s, v, m_prev, l_prev, acc_prev):
    m_new = jnp.maximum(m_prev, jnp.max(s, axis=-1, keepdims=True))
    corr = jnp.exp2(m_prev - m_new)
    p = jnp.exp2(s - m_new)
    l_new = l_prev * corr + _lane_partial_sum(p)
    acc_new = acc_prev * corr + _dot(p.astype(BF16), v)
    return m_new, l_new, acc_new


def _softmax_finish(l, acc):
    return acc / jnp.sum(l, axis=-1, keepdims=True)


PAIR_FIRST, PAIR_LAST, PAIR_MASKED = 1, 2, 4
ATTN_GROUP_ROWS = 256


def _prompt_attn_kernel(it_ref, jt_ref, fl_ref, q_ref, k_ref, o_ref, m_ref, l_ref, acc_ref,
                        *, tq, tk):
    p = pl.program_id(1)
    i = it_ref[p]
    j = jt_ref[p]
    flags = fl_ref[p]
    hg = max(1, ATTN_GROUP_ROWS // tq)
    n_groups = MLA_HEADS // hg
    rows = hg * tq

    @pl.when((flags & PAIR_FIRST) != 0)
    def _():
        m_ref[...] = jnp.full(m_ref.shape, NEG_INF, F32)
        l_ref[...] = jnp.zeros(l_ref.shape, F32)
        acc_ref[...] = jnp.zeros(acc_ref.shape, F32)

    def step(masked):
        k = k_ref[...]
        v = k[:, :KV_LORA]
        if masked:
            qpos = i * tq + lax.broadcasted_iota(jnp.int32, (tq, tk), 0)
            kpos = j * tk + lax.broadcasted_iota(jnp.int32, (tq, tk), 1)
            keep = (kpos <= qpos)[None]
        def scores(g):
            q = q_ref[g * hg:(g + 1) * hg].reshape(rows, K_CAT)
            s = _dot_nt(q, k)
            if masked:
                s = jnp.where(keep, s.reshape(hg, tq, tk), NEG_INF).reshape(rows, tk)
            return s

        s_next = scores(0)
        for g in range(n_groups):
            rs = slice(g * rows, (g + 1) * rows)
            s = s_next
            if g + 1 < n_groups:
                s_next = scores(g + 1)
            m_new, l_new, acc_new = _softmax_update(s, v, m_ref[rs], l_ref[rs], acc_ref[rs])
            m_ref[rs] = m_new
            l_ref[rs] = l_new
            acc_ref[rs] = acc_new

    @pl.when((flags & PAIR_MASKED) == 0)
    def _():
        step(False)

    @pl.when((flags & PAIR_MASKED) != 0)
    def _():
        step(True)

    @pl.when((flags & PAIR_LAST) != 0)
    def _():
        o = _softmax_finish(l_ref[...], acc_ref[...])
        o_ref[...] = o.reshape(MLA_HEADS, tq, KV_LORA).astype(o_ref.dtype)


def prompt_attention(q_hm, kcat, batch, seq, tq, tk):
    nq = seq // tq
    nk = seq // tk
    rows = MLA_HEADS * tq
    it, jt, fl = [], [], []
    for i in range(nq):
        last_j = (i * tq + tq - 1) // tk
        for j in range(last_j + 1):
            masked = j * tk + tk - 1 > i * tq
            it.append(i)
            jt.append(j)
            fl.append((PAIR_FIRST if j == 0 else 0) | (PAIR_LAST if j == last_j else 0)
                      | (PAIR_MASKED if masked else 0))
    tables = [jnp.asarray(t, jnp.int32) for t in (it, jt, fl)]
    q_map = lambda b, p, it, jt, fl: (0, b * nq + it[p], 0)
    kern = functools.partial(_prompt_attn_kernel, tq=tq, tk=tk)
    grid_spec = pltpu.PrefetchScalarGridSpec(
        num_scalar_prefetch=3,
        grid=(batch, len(it)),
        in_specs=[pl.BlockSpec((MLA_HEADS, tq, K_CAT), q_map),
                  pl.BlockSpec((tk, K_CAT), lambda b, p, it, jt, fl: (b * nk + jt[p], 0))],
        out_specs=pl.BlockSpec((MLA_HEADS, tq, KV_LORA), q_map),
        scratch_shapes=[pltpu.VMEM((rows, 1), F32),
                        pltpu.VMEM((rows, LANES), F32),
                        pltpu.VMEM((rows, KV_LORA), F32)],
    )
    return pl.pallas_call(
        kern,
        grid_spec=grid_spec,
        out_shape=jax.ShapeDtypeStruct((MLA_HEADS, batch * seq, KV_LORA), BF16),
        compiler_params=_params("parallel", "arbitrary"),
        name="prompt_attention",
    )(*tables, q_hm, kcat)


RING_SLOTS = 2


def _sample_attn_kernel(pt_ref, q_ref, ckv_hbm, kpe_hbm, ckv_new_ref, kpe_new_ref, o_ref,
                        cring_ref, rring_ref, sem, cbuf_ref, rbuf_ref, knew_ref,
                        m_ref, l_ref, acc_ref, *, group, pages_per_step, t_new):
    j = pl.program_id(1)
    n_j = pl.num_programs(1)
    step = pl.program_id(0) * n_j + j
    n_steps = pl.num_programs(0) * n_j
    slot = lax.rem(step, RING_SLOTS)
    rows = MLA_HEADS * t_new

    def page_copies(at_step, at_slot, g):
        b_at = lax.div(at_step, n_j)
        j_at = lax.rem(at_step, n_j)
        out = []
        for u in range(pages_per_step):
            page = pt_ref[b_at * group + g, j_at * pages_per_step + u]
            idx = g * pages_per_step + u
            out.append(pltpu.make_async_copy(ckv_hbm.at[page], cring_ref.at[at_slot, idx],
                                             sem.at[at_slot, g]))
            out.append(pltpu.make_async_copy(kpe_hbm.at[page], rring_ref.at[at_slot, idx],
                                             sem.at[at_slot, g]))
        return out

    def start_all(copies):
        for n, c in enumerate(copies):
            c.start(priority=(n // 2 + n) % 2)

    @pl.when(step == 0)
    def _():
        for g in range(group):
            start_all(page_copies(step, slot, g))

    @pl.when(j == 0)
    def _():
        m_ref[...] = jnp.full(m_ref.shape, NEG_INF, F32)
        l_ref[...] = jnp.zeros(l_ref.shape, F32)
        acc_ref[...] = jnp.zeros(acc_ref.shape, F32)

    def query(g):
        return q_ref[:, g * t_new:(g + 1) * t_new, :].reshape(rows, K_CAT).astype(BF16)

    def scores(g):
        @pl.when(step + 1 < n_steps)
        def _():
            start_all(page_copies(step + 1, RING_SLOTS - 1 - slot, g))

        for c in page_copies(step, slot, g):
            c.wait()
        for u in range(pages_per_step):
            ps = slice(u * PAGE_SIZE, (u + 1) * PAGE_SIZE)
            idx = g * pages_per_step + u
            cbuf_ref[g, ps, :] = cring_ref[slot, idx].astype(BF16)
            rbuf_ref[g, :, ps] = rring_ref[slot, idx].astype(BF16)
        q = query(g)
        return _dot_nt(q[:, :KV_LORA], cbuf_ref[g]) + _dot(q[:, KV_LORA:], rbuf_ref[g])

    s_next = scores(0)
    for g in range(group):
        s = s_next
        if g + 1 < group:
            s_next = scores(g + 1)
        m_new, l_new, acc_new = _softmax_update(s, cbuf_ref[g], m_ref[g], l_ref[g], acc_ref[g])
        m_ref[g] = m_new
        l_ref[g] = l_new
        acc_ref[g] = acc_new

    @pl.when(j == pl.num_programs(1) - 1)
    def _():
        qpos = lax.broadcasted_iota(jnp.int32, (t_new, PAGE_SIZE), 0)
        kpos = lax.broadcasted_iota(jnp.int32, (t_new, PAGE_SIZE), 1)
        keep = (kpos <= qpos)[None]
        for g in range(group):
            ts = slice(g * t_new, (g + 1) * t_new)
            knew_ref[...] = jnp.zeros(knew_ref.shape, F32)
            knew_ref[:t_new, :KV_LORA] = ckv_new_ref[ts, :]
            knew_ref[:t_new, KV_LORA:K_CAT] = kpe_new_ref[ts, :]
            kn = knew_ref[...].astype(BF16)
            sn = _dot_nt(query(g), kn)
            sn = jnp.where(keep, sn.reshape(MLA_HEADS, t_new, PAGE_SIZE), NEG_INF)
            _, l_new, acc_new = _softmax_update(sn.reshape(rows, PAGE_SIZE), kn[:, :KV_LORA],
                                                m_ref[g], l_ref[g], acc_ref[g])
            o = _softmax_finish(l_new, acc_new)
            o_ref[:, ts, :] = o.reshape(MLA_HEADS, t_new, KV_LORA).astype(o_ref.dtype)


def sample_attention(q_hm, cache_ckv, cache_kpe_t, page_table, ckv_new, kpe_new, t_new,
                     group, pages_per_step):
    dec_batch, n_pages = page_table.shape
    n_steps = n_pages // pages_per_step
    rows = MLA_HEADS * t_new
    gt = group * t_new

    in_specs = [pl.BlockSpec((MLA_HEADS, gt, K_CAT), lambda b, j, pt: (0, b, 0)),
                pl.BlockSpec(memory_space=pl.ANY),
                pl.BlockSpec(memory_space=pl.ANY),
                pl.BlockSpec((gt, KV_LORA), lambda b, j, pt: (b, 0)),
                pl.BlockSpec((gt, QK_ROPE), lambda b, j, pt: (b, 0))]
    kern = functools.partial(_sample_attn_kernel, group=group,
                             pages_per_step=pages_per_step, t_new=t_new)
    keys = pages_per_step * PAGE_SIZE
    n_slots = group * pages_per_step
    grid_spec = pltpu.PrefetchScalarGridSpec(
        num_scalar_prefetch=1,
        grid=(dec_batch // group, n_steps),
        in_specs=in_specs,
        out_specs=pl.BlockSpec((MLA_HEADS, gt, KV_LORA), lambda b, j, pt: (0, b, 0)),
        scratch_shapes=[pltpu.VMEM((RING_SLOTS, n_slots, PAGE_SIZE, KV_LORA), F32),
                        pltpu.VMEM((RING_SLOTS, n_slots, QK_ROPE, PAGE_SIZE), F32),
                        pltpu.SemaphoreType.DMA((RING_SLOTS, group)),
                        pltpu.VMEM((group, keys, KV_LORA), BF16),
                        pltpu.VMEM((group, QK_ROPE, keys), BF16),
                        pltpu.VMEM((PAGE_SIZE, K_CAT), F32),
                        pltpu.VMEM((group, rows, 1), F32),
                        pltpu.VMEM((group, rows, LANES), F32),
                        pltpu.VMEM((group, rows, KV_LORA), F32)],
    )
    return pl.pallas_call(
        kern,
        grid_spec=grid_spec,
        out_shape=jax.ShapeDtypeStruct((MLA_HEADS, dec_batch * t_new, KV_LORA), F32),
        compiler_params=_params("arbitrary", "arbitrary"),
        name="sample_attention",
    )(page_table, q_hm, cache_ckv, cache_kpe_t, ckv_new, kpe_new)


def _uv_kernel(o_ref, w_ref, out_ref):
    for h in range(MLA_HEADS):
        out_ref[:, h * V_HEAD:(h + 1) * V_HEAD] = _dot(o_ref[h].astype(BF16), w_ref[h])


def value_up_proj(o_lat, w_uv, tm):
    heads, m, _ = o_lat.shape
    return pl.pallas_call(
        _uv_kernel,
        grid=(m // tm,),
        in_specs=[pl.BlockSpec((heads, tm, KV_LORA), lambda i: (0, i, 0)),
                  pl.BlockSpec((heads, KV_LORA, V_HEAD), lambda i: (0, 0, 0))],
        out_specs=pl.BlockSpec((tm, heads * V_HEAD), lambda i: (i, 0)),
        out_shape=jax.ShapeDtypeStruct((m, heads * V_HEAD), F32),
        compiler_params=_params("parallel"),
        name="value_up_proj",
    )(o_lat, w_uv)


def _retention_kernel(*refs, chunk, has_s0):
    if has_s0:
        (q_ref, k_ref, v_ref, rg_ref, gm_ref, gr_ref, om_ref, rn_ref, s0_ref,
         mix_ref, sout_ref, s_ref) = refs
    else:
        (q_ref, k_ref, v_ref, rg_ref, gm_ref, gr_ref, om_ref, rn_ref,
         mix_ref, sout_ref, s_ref) = refs
    c = pl.program_id(1)

    @pl.when(c == 0)
    def _():
        if has_s0:
            s_ref[...] = s0_ref[0]
        else:
            s_ref[...] = jnp.zeros(s_ref.shape, F32)

    ii = lax.broadcasted_iota(jnp.int32, (chunk, chunk), 0)
    jj = lax.broadcasted_iota(jnp.int32, (chunk, chunk), 1)
    diff = (ii - jj).astype(F32)
    idx = lax.broadcasted_iota(jnp.int32, (chunk, 1), 0).astype(F32)
    for h in range(RET_HEADS):
        lg = math.log(1.0 - 2.0 ** (-5.0 - h))
        dmat = jnp.where(diff >= 0, jnp.exp(jnp.maximum(diff, 0.0) * lg), 0.0)
        q_dec = jnp.exp((idx + 1.0) * lg)
        k_dec = jnp.exp((chunk - 1.0 - idx) * lg)
        c_dec = math.exp(chunk * lg)
        sl = slice(h * RET_DK, (h + 1) * RET_DK)
        q = q_ref[:, sl].astype(BF16)
        k = k_ref[:, sl]
        v = v_ref[:, sl].astype(BF16)
        state = s_ref[h]
        a = _dot_nt(q, k.astype(BF16)) * dmat
        o = _dot(a.astype(BF16), v) + _dot(q, state.astype(BF16)) * q_dec
        kd = (k.astype(F32) * k_dec).astype(BF16)
        s_ref[h] = state * c_dec + _dot_tn(kd, v)
        mu = jnp.mean(o, axis=-1, keepdims=True)
        d = o - mu
        var = jnp.mean(d * d, axis=-1, keepdims=True)
        on = d * lax.rsqrt(var + EPS) * rn_ref[:, sl]
        rg = rg_ref[:, sl]
        o_ret = rg * _sigmoid(rg) * on
        mixed = _sigmoid(gm_ref[:, sl]) * om_ref[:, sl] + _sigmoid(gr_ref[:, sl]) * o_ret
        mix_ref[:, sl] = mixed.astype(mix_ref.dtype)

    @pl.when(c == pl.num_programs(1) - 1)
    def _():
        sout_ref[0] = s_ref[...]


def retention_mix(qk, v, gates, o_mla, ret_norm, s0, batch, n_chunks, chunk, mix_dtype):
    m = v.shape[0]
    has_s0 = s0 is not None
    row = lambda col: (lambda b, c: (b * n_chunks + c, col))
    blk = lambda col: pl.BlockSpec((chunk, RET_WIDTH), row(col))
    state_spec = pl.BlockSpec((1, RET_HEADS, RET_DK, RET_DV), lambda b, c: (b, 0, 0, 0))
    in_specs = [blk(0), blk(1), blk(0), blk(0), blk(1), blk(2), blk(0),
                pl.BlockSpec((1, RET_WIDTH), lambda b, c: (0, 0))]
    args = [qk, qk, v, gates, gates, gates, o_mla, ret_norm.reshape(1, RET_WIDTH)]
    if has_s0:
        in_specs.append(state_spec)
        args.append(s0)
    kern = functools.partial(_retention_kernel, chunk=chunk, has_s0=has_s0)
    return pl.pallas_call(
        kern,
        grid=(batch, n_chunks),
        in_specs=in_specs,
        out_specs=[blk(0), state_spec],
        out_shape=[jax.ShapeDtypeStruct((m, RET_WIDTH), mix_dtype),
                   jax.ShapeDtypeStruct((batch, RET_HEADS, RET_DK, RET_DV), F32)],
        scratch_shapes=[pltpu.VMEM((RET_HEADS, RET_DK, RET_DV), F32)],
        compiler_params=_params("parallel", "arbitrary"),
        name="retention_mix",
    )(*args)


def _out_proj_kernel(mix_ref, w_ref, x_ref, gpost_ref, gpre_ref, x1_ref, f_ref):
    y = _dot(mix_ref[...].astype(BF16), w_ref[...])
    y = y * lax.rsqrt(jnp.mean(y * y, axis=-1, keepdims=True) + EPS) * gpost_ref[...]
    x1 = x_ref[...] + y
    x1_ref[...] = x1
    f = x1 * lax.rsqrt(jnp.mean(x1 * x1, axis=-1, keepdims=True) + EPS) * gpre_ref[...]
    f_ref[...] = f.astype(f_ref.dtype)


def out_proj(mixed, w_out, x, g_post, g_pre, tm):
    m, d = x.shape
    row = lambda i: (i, 0)
    vec = pl.BlockSpec((1, d), lambda i: (0, 0))
    return pl.pallas_call(
        _out_proj_kernel,
        grid=(m // tm,),
        in_specs=[pl.BlockSpec((tm, d), row), pl.BlockSpec((d, d), lambda i: (0, 0)),
                  pl.BlockSpec((tm, d), row), vec, vec],
        out_specs=[pl.BlockSpec((tm, d), row), pl.BlockSpec((tm, d), row)],
        out_shape=[jax.ShapeDtypeStruct((m, d), F32), jax.ShapeDtypeStruct((m, d), BF16)],
        compiler_params=_params("parallel"),
        name="out_proj",
    )(mixed, w_out, x, g_post.reshape(1, d), g_pre.reshape(1, d))


def _ffn_up_kernel(f_ref, wg_ref, wu_ref, o_ref):
    f = f_ref[...]
    g = _dot(f, wg_ref[...])
    u = _dot(f, wu_ref[...])
    o_ref[...] = (g * _sigmoid(g) * u).astype(o_ref.dtype)


def ffn_up(f, w_gate, w_up, tm, tn):
    m, d = f.shape
    n = w_gate.shape[1]
    return pl.pallas_call(
        _ffn_up_kernel,
        grid=(m // tm, n // tn),
        in_specs=[pl.BlockSpec((tm, d), lambda i, j: (i, 0)),
                  pl.BlockSpec((d, tn), lambda i, j: (0, j)),
                  pl.BlockSpec((d, tn), lambda i, j: (0, j))],
        out_specs=pl.BlockSpec((tm, tn), lambda i, j: (i, j)),
        out_shape=jax.ShapeDtypeStruct((m, n), BF16),
        compiler_params=_params("parallel", "parallel"),
        name="ffn_up",
    )(f, w_gate, w_up)


def _ffn_down_kernel(a_ref, w_ref, x_ref, g_ref, o_ref):
    kk = pl.program_id(1)

    @pl.when(kk == 0)
    def _():
        o_ref[...] = jnp.zeros(o_ref.shape, F32)

    o_ref[...] += _dot(a_ref[...], w_ref[...])

    @pl.when(kk == pl.num_programs(1) - 1)
    def _():
        y = o_ref[...]
        y = y * lax.rsqrt(jnp.mean(y * y, axis=-1, keepdims=True) + EPS) * g_ref[...]
        o_ref[...] = x_ref[...] + y


def ffn_down(act, w_down, x1, g_post, tm, tk):
    m, dff = act.shape
    d = w_down.shape[1]
    return pl.pallas_call(
        _ffn_down_kernel,
        grid=(m // tm, dff // tk),
        in_specs=[pl.BlockSpec((tm, tk), lambda i, k: (i, k)),
                  pl.BlockSpec((tk, d), lambda i, k: (k, 0)),
                  pl.BlockSpec((tm, d), lambda i, k: (i, 0)),
                  pl.BlockSpec((1, d), lambda i, k: (0, 0))],
        out_specs=pl.BlockSpec((tm, d), lambda i, k: (i, 0)),
        out_shape=jax.ShapeDtypeStruct((m, d), F32),
        compiler_params=_params("parallel", "arbitrary"),
        name="ffn_down",
    )(act, w_down, x1, g_post.reshape(1, d))


def _rope_tables(pos, half):
    inv = ROPE_THETA ** (-jnp.arange(half, dtype=F32) / half)
    ang = pos.astype(F32)[:, None] * inv[None, :]
    return jnp.cos(ang), jnp.sin(ang)


def _tile_rows(table, rows):
    reps = max(1, rows // table.shape[0])
    return jnp.tile(table, (reps, 1))


def _prep_weights(w_in, w_uq, w_uk, w_uv, w_out, w_gate, w_up, w_down):
    offs = [0]
    for sz in IN_SIZES:
        offs.append(offs[-1] + sz)
    col = lambda a, b: w_in[:, offs[a]:offs[b]]
    k_pe = col(2, 3)
    half = QK_ROPE // 2
    k_pe_swap = jnp.concatenate([k_pe[:, half:], k_pe[:, :half]], axis=1)
    zpad = jnp.zeros((w_in.shape[0], LANES - QK_ROPE), w_in.dtype)
    w_lat = jnp.concatenate([col(0, 2), k_pe, zpad, k_pe_swap, zpad], axis=1)
    q_nope = w_uq[:, :, :QK_NOPE].reshape(Q_LORA, MLA_HEADS * QK_NOPE)
    q_rope = w_uq[:, :, QK_NOPE:]
    q_rope_swap = jnp.concatenate([q_rope[:, :, half:], q_rope[:, :, :half]], axis=2)
    w_q = jnp.concatenate([q_nope, q_rope.reshape(Q_LORA, -1), q_rope_swap.reshape(Q_LORA, -1)], axis=1)
    return {
        'w_qk': col(3, 5).astype(BF16),
        'w_v': col(5, 6).astype(BF16),
        'w_gates': col(6, 9).astype(BF16),
        'w_lat': w_lat.astype(BF16),
        'w_q': w_q.astype(BF16),
        'w_uk_t': jnp.swapaxes(w_uk, 1, 2).astype(BF16),
        'w_uv': w_uv.astype(BF16),
        'w_out': w_out.astype(BF16),
        'w_gate': w_gate.astype(BF16),
        'w_up': w_up.astype(BF16),
        'w_down': w_down.astype(BF16),
    }


def _row_tile(m, cap):
    t = min(m, cap)
    assert m % t == 0
    return t


def _layer(x, pos, attend, s0, batch, chunk, narrow_dtype, w, norms):
    m = x.shape[0]
    t_len = m // batch
    tm = _row_tile(m, 1024)
    tm_small = _row_tile(m, 512)

    cos_r, sin_r = _rope_tables(pos, RET_DK // 2)
    cos_r, sin_r = _tile_rows(cos_r, tm), _tile_rows(sin_r, tm)
    cos_m, sin_m = _rope_tables(pos, QK_ROPE // 2)
    reps = LANES // (QK_ROPE // 2)
    cos_m = _tile_rows(jnp.tile(cos_m, (1, reps)), tm_small)
    sin_m = _tile_rows(jnp.tile(jnp.concatenate([-sin_m, sin_m], axis=1), (1, reps // 2)), tm_small)

    h = rmsnorm_rows(x, norms['norm_mix_pre'], tm_small)
    qk = matmul_rope(h, w['w_qk'], cos_r, sin_r, narrow_dtype, tm, 512)
    v = matmul(h, w['w_v'], narrow_dtype, tm, 512)
    gates = matmul(h, w['w_gates'], F32, tm, 512)
    cq, ckv, kpe, kcat = latent_proj(h, w['w_lat'], norms['q_norm'], norms['kv_norm'],
                                     cos_m, sin_m, tm_small)
    q_hm = query_proj(cq, w['w_q'], w['w_uk_t'], cos_m, sin_m, narrow_dtype, tm_small)
    o_lat = attend(q_hm, kcat, ckv, kpe)
    o_mla = value_up_proj(o_lat, w['w_uv'], _row_tile(m, 256))
    mixed, s_new = retention_mix(qk, v, gates, o_mla, norms['ret_norm'], s0,
                                 batch, t_len // chunk, chunk, narrow_dtype)
    x1, f = out_proj(mixed, w['w_out'], x, norms['norm_mix_post'], norms['norm_ffn_pre'],
                     _row_tile(m, 256))
    act = ffn_up(f, w['w_gate'], w['w_up'], tm, 512)
    y = ffn_down(act, w['w_down'], x1, norms['norm_ffn_post'], tm, 512)
    return y, ckv, kpe, s_new


def kernel(x_prompt, x_sample, cache_kv_latent, cache_k_rope, page_table, state_retention,
           w_in, q_norm, kv_norm, w_uq, w_uk, w_uv, ret_norm, w_out,
           norm_mix_pre, norm_mix_post, norm_ffn_pre, norm_ffn_post, w_gate, w_up, w_down):
    b, s_len, d = x_prompt.shape
    db, t_s, _ = x_sample.shape
    depth = w_in.shape[0]
    past_len = page_table.shape[1] * PAGE_SIZE
    pos_p = jnp.arange(s_len, dtype=jnp.int32)
    pos_s = past_len + jnp.arange(t_s, dtype=jnp.int32)
    pages_per_step = math.gcd(page_table.shape[1], 8)
    group = math.gcd(db, 4)

    yp = x_prompt.reshape(b * s_len, d)
    ys = x_sample.reshape(db * t_s, d)
    outs = [[] for _ in range(6)]
    for l in range(depth):
        w = _prep_weights(w_in[l], w_uq[l], w_uk[l], w_uv[l], w_out[l], w_gate[l], w_up[l], w_down[l])
        norms = {'q_norm': q_norm[l], 'kv_norm': kv_norm[l], 'ret_norm': ret_norm[l],
                 'norm_mix_pre': norm_mix_pre[l], 'norm_mix_post': norm_mix_post[l],
                 'norm_ffn_pre': norm_ffn_pre[l], 'norm_ffn_post': norm_ffn_post[l]}

        def attend_p(q_hm, kcat, ckv, kpe):
            return prompt_attention(q_hm, kcat, b, s_len, min(256, s_len), min(512, s_len))

        def attend_s(q_hm, kcat, ckv, kpe, l=l):
            cache_kpe_t = jnp.swapaxes(cache_k_rope[l], 1, 2)
            return sample_attention(q_hm, cache_kv_latent[l], cache_kpe_t, page_table,
                                    ckv, kpe, t_s, group, pages_per_step)

        yp, c1, k1, s1 = _layer(yp, pos_p, attend_p, None, b, min(RET_CHUNK, s_len), BF16, w, norms)
        ys, c2, k2, s2 = _layer(ys, pos_s, attend_s, state_retention[l], db, t_s, F32, w, norms)
        outs[0].append(c1.reshape(b, s_len, KV_LORA))
        outs[1].append(k1.reshape(b, s_len, QK_ROPE))
        outs[2].append(s1)
        outs[3].append(c2.reshape(db, t_s, KV_LORA))
        outs[4].append(k2.reshape(db, t_s, QK_ROPE))
        outs[5].append(s2)
    return (yp.reshape(b, s_len, d), ys.reshape(db, t_s, d)) + tuple(jnp.stack(o) for o in outs)
```

```python
import functools
import math

import jax
import jax.numpy as jnp
from jax import lax
from jax.experimental import pallas as pl
from jax.experimental.pallas import tpu as pltpu

F32 = jnp.float32
BF16 = jnp.bfloat16

D_MODEL = 2048
PAGE_SIZE = 128
MLA_HEADS = 16
Q_LORA = 512
KV_LORA = 512
QK_NOPE = 128
QK_ROPE = 64
V_HEAD = 128
MLA_WIDTH = MLA_HEADS * V_HEAD
MLA_SCALE = (QK_NOPE + QK_ROPE) ** -0.5
SCORE_SCALE_LOG2 = MLA_SCALE * math.log2(math.e)
K_CAT = KV_LORA + QK_ROPE
RET_HEADS = 8
RET_DK = D_MODEL // RET_HEADS
RET_DV = D_MODEL // RET_HEADS
RET_WIDTH = RET_HEADS * RET_DV
RET_K_SCALE = RET_DK ** -0.5
RET_CHUNK = 128
ROPE_THETA = 10000.0
EPS = 1e-6
NEG_INF = -1e30
IN_SIZES = (Q_LORA, KV_LORA, QK_ROPE, RET_HEADS * RET_DK, RET_HEADS * RET_DK,
            RET_WIDTH, RET_WIDTH, MLA_WIDTH, RET_WIDTH)

LANES = 128
VMEM_LIMIT = 56 * 1024 * 1024


def _params(*sem):
    return pltpu.CompilerParams(dimension_semantics=sem, vmem_limit_bytes=VMEM_LIMIT)


def _sigmoid(x):
    return 0.5 * jnp.tanh(0.5 * x) + 0.5


def _dot(a, b):
    return jnp.dot(a, b, preferred_element_type=F32)


def _dot_nt(a, b):
    return lax.dot_general(a, b, (((1,), (1,)), ((), ())), preferred_element_type=F32)


def _dot_tn(a, b):
    return lax.dot_general(a, b, (((0,), (0,)), ((), ())), preferred_element_type=F32)


def _rmsnorm_kernel(x_ref, g_ref, o_ref):
    x = x_ref[...]
    ms = jnp.mean(x * x, axis=-1, keepdims=True)
    o_ref[...] = (x * lax.rsqrt(ms + EPS) * g_ref[...]).astype(o_ref.dtype)


def rmsnorm_rows(x, g, tm):
    m, d = x.shape
    return pl.pallas_call(
        _rmsnorm_kernel,
        grid=(m // tm,),
        in_specs=[pl.BlockSpec((tm, d), lambda i: (i, 0)),
                  pl.BlockSpec((1, d), lambda i: (0, 0))],
        out_specs=pl.BlockSpec((tm, d), lambda i: (i, 0)),
        out_shape=jax.ShapeDtypeStruct((m, d), BF16),
        compiler_params=_params("parallel"),
        name="rmsnorm_rows",
    )(x, g.reshape(1, d))


def _mm_kernel(a_ref, w_ref, o_ref):
    o_ref[...] = _dot(a_ref[...], w_ref[...]).astype(o_ref.dtype)


def matmul(a, w, out_dtype, tm, tn):
    m, k = a.shape
    n = w.shape[1]
    return pl.pallas_call(
        _mm_kernel,
        grid=(m // tm, n // tn),
        in_specs=[pl.BlockSpec((tm, k), lambda i, j: (i, 0)),
                  pl.BlockSpec((k, tn), lambda i, j: (0, j))],
        out_specs=pl.BlockSpec((tm, tn), lambda i, j: (i, j)),
        out_shape=jax.ShapeDtypeStruct((m, n), out_dtype),
        compiler_params=_params("parallel", "parallel"),
        name="matmul",
    )(a, w)


def _mm_rope_kernel(a_ref, w_ref, cos_ref, sin_ref, o_ref, *, n_q_tiles, heads_per_tile):
    acc = _dot(a_ref[...], w_ref[...])
    c = cos_ref[...]
    s = sin_ref[...]
    scale = jnp.where(pl.program_id(1) >= n_q_tiles, RET_K_SCALE, 1.0).astype(F32)
    half = RET_DK // 2
    for h in range(heads_per_tile):
        lo = h * RET_DK
        x1 = acc[:, lo:lo + half]
        x2 = acc[:, lo + half:lo + RET_DK]
        o_ref[:, lo:lo + half] = ((x1 * c - x2 * s) * scale).astype(o_ref.dtype)
        o_ref[:, lo + half:lo + RET_DK] = ((x1 * s + x2 * c) * scale).astype(o_ref.dtype)


def matmul_rope(a, w, cos, sin, out_dtype, tm, tn):
    m, k = a.shape
    n = w.shape[1]
    pos_blocks = cos.shape[0] // tm
    kern = functools.partial(_mm_rope_kernel, n_q_tiles=(n // 2) // tn,
                             heads_per_tile=tn // RET_DK)
    return pl.pallas_call(
        kern,
        grid=(m // tm, n // tn),
        in_specs=[pl.BlockSpec((tm, k), lambda i, j: (i, 0)),
                  pl.BlockSpec((k, tn), lambda i, j: (0, j)),
                  pl.BlockSpec((tm, RET_DK // 2), lambda i, j: (i % pos_blocks, 0)),
                  pl.BlockSpec((tm, RET_DK // 2), lambda i, j: (i % pos_blocks, 0))],
        out_specs=pl.BlockSpec((tm, tn), lambda i, j: (i, j)),
        out_shape=jax.ShapeDtypeStruct((m, n), out_dtype),
        compiler_params=_params("parallel", "parallel"),
        name="matmul_rope",
    )(a, w, cos, sin)


def _latent_kernel(a_ref, w_ref, qn_ref, kvn_ref, cos_ref, sin_ref,
                   cq_ref, ckv_ref, kpe_ref, kcat_ref):
    acc = _dot(a_ref[...], w_ref[...])
    cq = acc[:, :Q_LORA]
    cq = cq * lax.rsqrt(jnp.mean(cq * cq, axis=-1, keepdims=True) + EPS) * qn_ref[...]
    cq_ref[...] = cq.astype(cq_ref.dtype)
    ckv = acc[:, Q_LORA:Q_LORA + KV_LORA]
    ckv = ckv * lax.rsqrt(jnp.mean(ckv * ckv, axis=-1, keepdims=True) + EPS) * kvn_ref[...]
    ckv_ref[...] = ckv
    base = Q_LORA + KV_LORA
    x = acc[:, base:base + LANES]
    xs = acc[:, base + LANES:base + 2 * LANES]
    kpe = (x * cos_ref[...] + xs * sin_ref[...])[:, :QK_ROPE]
    kpe_ref[...] = kpe
    kcat_ref[:, :KV_LORA] = ckv.astype(kcat_ref.dtype)
    kcat_ref[:, KV_LORA:K_CAT] = kpe.astype(kcat_ref.dtype)


def latent_proj(a, w, q_norm, kv_norm, cos, sin, tm):
    m, k = a.shape
    n = w.shape[1]
    pos_blocks = cos.shape[0] // tm
    row = lambda i: (i, 0)
    return pl.pallas_call(
        _latent_kernel,
        grid=(m // tm,),
        in_specs=[pl.BlockSpec((tm, k), row),
                  pl.BlockSpec((k, n), lambda i: (0, 0)),
                  pl.BlockSpec((1, Q_LORA), lambda i: (0, 0)),
                  pl.BlockSpec((1, KV_LORA), lambda i: (0, 0)),
                  pl.BlockSpec((tm, LANES), lambda i: (i % pos_blocks, 0)),
                  pl.BlockSpec((tm, LANES), lambda i: (i % pos_blocks, 0))],
        out_specs=[pl.BlockSpec((tm, Q_LORA), row),
                   pl.BlockSpec((tm, KV_LORA), row),
                   pl.BlockSpec((tm, QK_ROPE), row),
                   pl.BlockSpec((tm, K_CAT), row)],
        out_shape=[jax.ShapeDtypeStruct((m, Q_LORA), BF16),
                   jax.ShapeDtypeStruct((m, KV_LORA), F32),
                   jax.ShapeDtypeStruct((m, QK_ROPE), F32),
                   jax.ShapeDtypeStruct((m, K_CAT), BF16)],
        compiler_params=_params("parallel"),
        name="latent_proj",
    )(a, w, q_norm.reshape(1, Q_LORA), kv_norm.reshape(1, KV_LORA), cos, sin)


def _query_kernel(cq_ref, wq_ref, wuk_ref, cos_ref, sin_ref, o_ref):
    q = _dot(cq_ref[...], wq_ref[...])
    for h in range(MLA_HEADS):
        qn = q[:, h * QK_NOPE:(h + 1) * QK_NOPE].astype(BF16)
        o_ref[h, :, :KV_LORA] = (_dot(qn, wuk_ref[h]) * SCORE_SCALE_LOG2).astype(o_ref.dtype)
    c = cos_ref[...] * SCORE_SCALE_LOG2
    s = sin_ref[...] * SCORE_SCALE_LOG2
    rope0 = MLA_HEADS * QK_NOPE
    swap0 = rope0 + MLA_HEADS * QK_ROPE
    for p in range(MLA_HEADS * QK_ROPE // LANES):
        x = q[:, rope0 + p * LANES:rope0 + (p + 1) * LANES]
        xs = q[:, swap0 + p * LANES:swap0 + (p + 1) * LANES]
        qp = x * c + xs * s
        o_ref[2 * p, :, KV_LORA:K_CAT] = qp[:, :QK_ROPE].astype(o_ref.dtype)
        o_ref[2 * p + 1, :, KV_LORA:K_CAT] = qp[:, QK_ROPE:].astype(o_ref.dtype)


def query_proj(cq, wq, wuk_t, cos, sin, out_dtype, tm):
    m = cq.shape[0]
    pos_blocks = cos.shape[0] // tm
    return pl.pallas_call(
        _query_kernel,
        grid=(m // tm,),
        in_specs=[pl.BlockSpec((tm, Q_LORA), lambda i: (i, 0)),
                  pl.BlockSpec(wq.shape, lambda i: (0, 0)),
                  pl.BlockSpec(wuk_t.shape, lambda i: (0, 0, 0)),
                  pl.BlockSpec((tm, LANES), lambda i: (i % pos_blocks, 0)),
                  pl.BlockSpec((tm, LANES), lambda i: (i % pos_blocks, 0))],
        out_specs=pl.BlockSpec((MLA_HEADS, tm, K_CAT), lambda i: (0, i, 0)),
        out_shape=jax.ShapeDtypeStruct((MLA_HEADS, m, K_CAT), out_dtype),
        compiler_params=_params("parallel"),
        name="query_proj",
    )(cq, wq, wuk_t, cos, sin)


def _lane_partial_sum(p):
    out = p[:, :LANES]
    for c in range(1, p.shape[1] // LANES):
        out = out + p[:, c * LANES:(c + 1) * LANES]
    return out


def _softmax_update(s, v, m_prev, l_prev, acc_prev):
    m_new = jnp.maximum(m_prev, jnp.max(s, axis=-1, keepdims=True))
    corr = jnp.exp2(m_prev - m_new)
    p = jnp.exp2(s - m_new)
    l_new = l_prev * corr + _lane_partial_sum(p)
    acc_new = acc_prev * corr + _dot(p.astype(BF16), v)
    return m_new, l_new, acc_new


def _softmax_finish(l, acc):
    return acc / jnp.sum(l, axis=-1, keepdims=True)


PAIR_FIRST, PAIR_LAST, PAIR_MASKED, PAIR_HALF = 1, 2, 4, 8
ATTN_GROUP_ROWS = 256


def _prompt_attn_kernel(it_ref, jt_ref, fl_ref, q_ref, k_ref, o_ref, m_ref, l_ref, acc_ref,
                        *, tq, tk):
    p = pl.program_id(1)
    i = it_ref[p]
    j = jt_ref[p]
    flags = fl_ref[p]
    hg = max(1, ATTN_GROUP_ROWS // tq)
    n_groups = MLA_HEADS // hg
    rows = hg * tq

    @pl.when((flags & PAIR_FIRST) != 0)
    def _():
        m_ref[...] = jnp.full(m_ref.shape, NEG_INF, F32)
        l_ref[...] = jnp.zeros(l_ref.shape, F32)
        acc_ref[...] = jnp.zeros(acc_ref.shape, F32)

    def step(masked, nk=tk):
        k = k_ref[:nk]
        v = k[:, :KV_LORA]
        if masked:
            qpos = i * tq + lax.broadcasted_iota(jnp.int32, (tq, nk), 0)
            kpos = j * tk + lax.broadcasted_iota(jnp.int32, (tq, nk), 1)
            keep = (kpos <= qpos)[None]
        def scores(g):
            q = q_ref[g * hg:(g + 1) * hg].reshape(rows, K_CAT)
            s = _dot_nt(q, k)
            if masked:
                s = jnp.where(keep, s.reshape(hg, tq, nk), NEG_INF).reshape(rows, nk)
            return s

        s_next = scores(0)
        for g in range(n_groups):
            rs = slice(g * rows, (g + 1) * rows)
            s = s_next
            if g + 1 < n_groups:
                s_next = scores(g + 1)
            m_new, l_new, acc_new = _softmax_update(s, v, m_ref[rs], l_ref[rs], acc_ref[rs])
            m_ref[rs] = m_new
            l_ref[rs] = l_new
            acc_ref[rs] = acc_new

    @pl.when((flags & PAIR_MASKED) == 0)
    def _():
        step(False)

    @pl.when((flags & (PAIR_MASKED | PAIR_HALF)) == PAIR_MASKED)
    def _():
        step(True)

    @pl.when((flags & PAIR_HALF) != 0)
    def _():
        step(True, tk // 2)

    @pl.when((flags & PAIR_LAST) != 0)
    def _():
        o = _softmax_finish(l_ref[...], acc_ref[...])
        o_ref[...] = o.reshape(MLA_HEADS, tq, KV_LORA).astype(o_ref.dtype)


def prompt_attention(q_hm, kcat, batch, seq, tq, tk):
    nq = seq // tq
    nk = seq // tk
    rows = MLA_HEADS * tq
    it, jt, fl = [], [], []
    for i in range(nq):
        last_j = (i * tq + tq - 1) // tk
        for j in range(last_j + 1):
            masked = j * tk + tk - 1 > i * tq
            half = masked and tk // 2 >= LANES and i * tq + tq - 1 < j * tk + tk // 2
            it.append(i)
            jt.append(j)
            fl.append((PAIR_FIRST if j == 0 else 0) | (PAIR_LAST if j == last_j else 0)
                      | (PAIR_MASKED if masked else 0) | (PAIR_HALF if half else 0))
    tables = [jnp.asarray(t, jnp.int32) for t in (it, jt, fl)]
    q_map = lambda b, p, it, jt, fl: (0, b * nq + it[p], 0)
    kern = functools.partial(_prompt_attn_kernel, tq=tq, tk=tk)
    grid_spec = pltpu.PrefetchScalarGridSpec(
        num_scalar_prefetch=3,
        grid=(batch, len(it)),
        in_specs=[pl.BlockSpec((MLA_HEADS, tq, K_CAT), q_map),
                  pl.BlockSpec((tk, K_CAT), lambda b, p, it, jt, fl: (b * nk + jt[p], 0))],
        out_specs=pl.BlockSpec((MLA_HEADS, tq, KV_LORA), q_map),
        scratch_shapes=[pltpu.VMEM((rows, 1), F32),
                        pltpu.VMEM((rows, LANES), F32),
                        pltpu.VMEM((rows, KV_LORA), F32)],
    )
    return pl.pallas_call(
        kern,
        grid_spec=grid_spec,
        out_shape=jax.ShapeDtypeStruct((MLA_HEADS, batch * seq, KV_LORA), BF16),
        compiler_params=_params("parallel", "arbitrary"),
        name="prompt_attention",
    )(*tables, q_hm, kcat)


RING_SLOTS = 2


def _sample_attn_kernel(pt_ref, q_ref, ckv_hbm, kpe_hbm, ckv_new_ref, kpe_new_ref, o_ref,
                        cring_ref, rring_ref, sem, cbuf_ref, rbuf_ref, knew_ref,
                        m_ref, l_ref, acc_ref, *, group, pages_per_step, t_new):
    j = pl.program_id(1)
    n_j = pl.num_programs(1)
    step = pl.program_id(0) * n_j + j
    n_steps = pl.num_programs(0) * n_j
    slot = lax.rem(step, RING_SLOTS)
    rows = MLA_HEADS * t_new

    def page_copies(at_step, at_slot, g):
        b_at = lax.div(at_step, n_j)
        j_at = lax.rem(at_step, n_j)
        out = []
        for u in range(pages_per_step):
            page = pt_ref[b_at * group + g, j_at * pages_per_step + u]
            idx = g * pages_per_step + u
            out.append(pltpu.make_async_copy(ckv_hbm.at[page], cring_ref.at[at_slot, idx],
                                             sem.at[at_slot, g]))
            out.append(pltpu.make_async_copy(kpe_hbm.at[page], rring_ref.at[at_slot, idx],
                                             sem.at[at_slot, g]))
        return out

    def start_all(copies):
        for n, c in enumerate(copies):
            c.start(priority=(n // 2 + n) % 2)

    @pl.when(step == 0)
    def _():
        for g in range(group):
            start_all(page_copies(step, slot, g))

    @pl.when(j == 0)
    def _():
        m_ref[...] = jnp.full(m_ref.shape, NEG_INF, F32)
        l_ref[...] = jnp.zeros(l_ref.shape, F32)
        acc_ref[...] = jnp.zeros(acc_ref.shape, F32)

    def query(g):
        return q_ref[:, g * t_new:(g + 1) * t_new, :].reshape(rows, K_CAT).astype(BF16)

    def scores(g):
        @pl.when(step + 1 < n_steps)
        def _():
            start_all(page_copies(step + 1, RING_SLOTS - 1 - slot, g))

        for c in page_copies(step, slot, g):
            c.wait()
        for u in range(pages_per_step):
            ps = slice(u * PAGE_SIZE, (u + 1) * PAGE_SIZE)
            idx = g * pages_per_step + u
            cbuf_ref[g, ps, :] = cring_ref[slot, idx].astype(BF16)
            rbuf_ref[g, :, ps] = rring_ref[slot, idx].astype(BF16)
        q = query(g)
        return _dot_nt(q[:, :KV_LORA], cbuf_ref[g]) + _dot(q[:, KV_LORA:], rbuf_ref[g])

    s_next = scores(0)
    for g in range(group):
        s = s_next
        if g + 1 < group:
            s_next = scores(g + 1)
        m_new, l_new, acc_new = _softmax_update(s, cbuf_ref[g], m_ref[g], l_ref[g], acc_ref[g])
        m_ref[g] = m_new
        l_ref[g] = l_new
        acc_ref[g] = acc_new

    @pl.when(j == pl.num_programs(1) - 1)
    def _():
        qpos = lax.broadcasted_iota(jnp.int32, (t_new, PAGE_SIZE), 0)
        kpos = lax.broadcasted_iota(jnp.int32, (t_new, PAGE_SIZE), 1)
        keep = (kpos <= qpos)[None]
        for g in range(group):
            ts = slice(g * t_new, (g + 1) * t_new)
            knew_ref[...] = jnp.zeros(knew_ref.shape, F32)
            knew_ref[:t_new, :KV_LORA] = ckv_new_ref[ts, :]
            knew_ref[:t_new, KV_LORA:K_CAT] = kpe_new_ref[ts, :]
            kn = knew_ref[...].astype(BF16)
            sn = _dot_nt(query(g), kn)
            sn = jnp.where(keep, sn.reshape(MLA_HEADS, t_new, PAGE_SIZE), NEG_INF)
            _, l_new, acc_new = _softmax_update(sn.reshape(rows, PAGE_SIZE), kn[:, :KV_LORA],
                                                m_ref[g], l_ref[g], acc_ref[g])
            o = _softmax_finish(l_new, acc_new)
            o_ref[:, ts, :] = o.reshape(MLA_HEADS, t_new, KV_LORA).astype(o_ref.dtype)


def sample_attention(q_hm, cache_ckv, cache_kpe_t, page_table, ckv_new, kpe_new, t_new,
                     group, pages_per_step):
    dec_batch, n_pages = page_table.shape
    n_steps = n_pages // pages_per_step
    rows = MLA_HEADS * t_new
    gt = group * t_new

    in_specs = [pl.BlockSpec((MLA_HEADS, gt, K_CAT), lambda b, j, pt: (0, b, 0)),
                pl.BlockSpec(memory_space=pl.ANY),
                pl.BlockSpec(memory_space=pl.ANY),
                pl.BlockSpec((gt, KV_LORA), lambda b, j, pt: (b, 0)),
                pl.BlockSpec((gt, QK_ROPE), lambda b, j, pt: (b, 0))]
    kern = functools.partial(_sample_attn_kernel, group=group,
                             pages_per_step=pages_per_step, t_new=t_new)
    keys = pages_per_step * PAGE_SIZE
    n_slots = group * pages_per_step
    grid_spec = pltpu.PrefetchScalarGridSpec(
        num_scalar_prefetch=1,
        grid=(dec_batch // group, n_steps),
        in_specs=in_specs,
        out_specs=pl.BlockSpec((MLA_HEADS, gt, KV_LORA), lambda b, j, pt: (0, b, 0)),
        scratch_shapes=[pltpu.VMEM((RING_SLOTS, n_slots, PAGE_SIZE, KV_LORA), F32),
                        pltpu.VMEM((RING_SLOTS, n_slots, QK_ROPE, PAGE_SIZE), F32),
                        pltpu.SemaphoreType.DMA((RING_SLOTS, group)),
                        pltpu.VMEM((group, keys, KV_LORA), BF16),
                        pltpu.VMEM((group, QK_ROPE, keys), BF16),
                        pltpu.VMEM((PAGE_SIZE, K_CAT), F32),
                        pltpu.VMEM((group, rows, 1), F32),
                        pltpu.VMEM((group, rows, LANES), F32),
                        pltpu.VMEM((group, rows, KV_LORA), F32)],
    )
    return pl.pallas_call(
        kern,
        grid_spec=grid_spec,
        out_shape=jax.ShapeDtypeStruct((MLA_HEADS, dec_batch * t_new, KV_LORA), F32),
        compiler_params=_params("arbitrary", "arbitrary"),
        name="sample_attention",
    )(page_table, q_hm, cache_ckv, cache_kpe_t, ckv_new, kpe_new)


def _uv_kernel(o_ref, w_ref, out_ref):
    for h in range(MLA_HEADS):
        out_ref[:, h * V_HEAD:(h + 1) * V_HEAD] = _dot(o_ref[h].astype(BF16),
                                                       w_ref[h]).astype(out_ref.dtype)


def value_up_proj(o_lat, w_uv, out_dtype, tm):
    heads, m, _ = o_lat.shape
    return pl.pallas_call(
        _uv_kernel,
        grid=(m // tm,),
        in_specs=[pl.BlockSpec((heads, tm, KV_LORA), lambda i: (0, i, 0)),
                  pl.BlockSpec((heads, KV_LORA, V_HEAD), lambda i: (0, 0, 0))],
        out_specs=pl.BlockSpec((tm, heads * V_HEAD), lambda i: (i, 0)),
        out_shape=jax.ShapeDtypeStruct((m, heads * V_HEAD), out_dtype),
        compiler_params=_params("parallel"),
        name="value_up_proj",
    )(o_lat, w_uv)


def _retention_kernel(*refs, chunk, has_s0):
    if has_s0:
        (q_ref, k_ref, v_ref, rg_ref, gm_ref, gr_ref, om_ref, rn_ref, s0_ref,
         mix_ref, sout_ref, s_ref) = refs
    else:
        (q_ref, k_ref, v_ref, rg_ref, gm_ref, gr_ref, om_ref, rn_ref,
         mix_ref, sout_ref, s_ref) = refs
    c = pl.program_id(1)

    @pl.when(c == 0)
    def _():
        if has_s0:
            s_ref[...] = s0_ref[0]
        else:
            s_ref[...] = jnp.zeros(s_ref.shape, F32)

    ii = lax.broadcasted_iota(jnp.int32, (chunk, chunk), 0)
    jj = lax.broadcasted_iota(jnp.int32, (chunk, chunk), 1)
    diff = (ii - jj).astype(F32)
    idx = lax.broadcasted_iota(jnp.int32, (chunk, 1), 0).astype(F32)
    for h in range(RET_HEADS):
        lg = math.log(1.0 - 2.0 ** (-5.0 - h))
        dmat = jnp.where(diff >= 0, jnp.exp(jnp.maximum(diff, 0.0) * lg), 0.0)
        q_dec = jnp.exp((idx + 1.0) * lg)
        k_dec = jnp.exp((chunk - 1.0 - idx) * lg)
        c_dec = math.exp(chunk * lg)
        sl = slice(h * RET_DK, (h + 1) * RET_DK)
        q = q_ref[:, sl].astype(BF16)
        k = k_ref[:, sl]
        v = v_ref[:, sl].astype(BF16)
        state = s_ref[h]
        a = _dot_nt(q, k.astype(BF16)) * dmat
        o = _dot(a.astype(BF16), v) + _dot(q, state.astype(BF16)) * q_dec
        kd = (k.astype(F32) * k_dec).astype(BF16)
        s_ref[h] = state * c_dec + _dot_tn(kd, v)
        mu = jnp.mean(o, axis=-1, keepdims=True)
        d = o - mu
        var = jnp.mean(d * d, axis=-1, keepdims=True)
        on = d * lax.rsqrt(var + EPS) * rn_ref[:, sl]
        rg = rg_ref[:, sl].astype(F32)
        o_ret = rg * _sigmoid(rg) * on
        mixed = (_sigmoid(gm_ref[:, sl].astype(F32)) * om_ref[:, sl].astype(F32)
                 + _sigmoid(gr_ref[:, sl].astype(F32)) * o_ret)
        mix_ref[:, sl] = mixed.astype(mix_ref.dtype)

    @pl.when(c == pl.num_programs(1) - 1)
    def _():
        sout_ref[0] = s_ref[...]


def retention_mix(qk, v, gates, o_mla, ret_norm, s0, batch, n_chunks, chunk, mix_dtype):
    m = v.shape[0]
    has_s0 = s0 is not None
    row = lambda col: (lambda b, c: (b * n_chunks + c, col))
    blk = lambda col: pl.BlockSpec((chunk, RET_WIDTH), row(col))
    state_spec = pl.BlockSpec((1, RET_HEADS, RET_DK, RET_DV), lambda b, c: (b, 0, 0, 0))
    in_specs = [blk(0), blk(1), blk(0), blk(0), blk(1), blk(2), blk(0),
                pl.BlockSpec((1, RET_WIDTH), lambda b, c: (0, 0))]
    args = [qk, qk, v, gates, gates, gates, o_mla, ret_norm.reshape(1, RET_WIDTH)]
    if has_s0:
        in_specs.append(state_spec)
        args.append(s0)
    kern = functools.partial(_retention_kernel, chunk=chunk, has_s0=has_s0)
    return pl.pallas_call(
        kern,
        grid=(batch, n_chunks),
        in_specs=in_specs,
        out_specs=[blk(0), state_spec],
        out_shape=[jax.ShapeDtypeStruct((m, RET_WIDTH), mix_dtype),
                   jax.ShapeDtypeStruct((batch, RET_HEADS, RET_DK, RET_DV), F32)],
        scratch_shapes=[pltpu.VMEM((RET_HEADS, RET_DK, RET_DV), F32)],
        compiler_params=_params("parallel", "arbitrary"),
        name="retention_mix",
    )(*args)


def _out_proj_kernel(mix_ref, w_ref, x_ref, gpost_ref, gpre_ref, x1_ref, f_ref):
    y = _dot(mix_ref[...].astype(BF16), w_ref[...])
    y = y * lax.rsqrt(jnp.mean(y * y, axis=-1, keepdims=True) + EPS) * gpost_ref[...]
    x1 = x_ref[...] + y
    x1_ref[...] = x1
    f = x1 * lax.rsqrt(jnp.mean(x1 * x1, axis=-1, keepdims=True) + EPS) * gpre_ref[...]
    f_ref[...] = f.astype(f_ref.dtype)


def out_proj(mixed, w_out, x, g_post, g_pre, tm):
    m, d = x.shape
    row = lambda i: (i, 0)
    vec = pl.BlockSpec((1, d), lambda i: (0, 0))
    return pl.pallas_call(
        _out_proj_kernel,
        grid=(m // tm,),
        in_specs=[pl.BlockSpec((tm, d), row), pl.BlockSpec((d, d), lambda i: (0, 0)),
                  pl.BlockSpec((tm, d), row), vec, vec],
        out_specs=[pl.BlockSpec((tm, d), row), pl.BlockSpec((tm, d), row)],
        out_shape=[jax.ShapeDtypeStruct((m, d), F32), jax.ShapeDtypeStruct((m, d), BF16)],
        compiler_params=_params("parallel"),
        name="out_proj",
    )(mixed, w_out, x, g_post.reshape(1, d), g_pre.reshape(1, d))


def _ffn_kernel(f_ref, wg_ref, wu_ref, wd_ref, x_ref, g_ref, o_ref):
    kk = pl.program_id(1)

    @pl.when(kk == 0)
    def _():
        o_ref[...] = jnp.zeros(o_ref.shape, F32)

    f = f_ref[...]
    g = _dot(f, wg_ref[...])
    u = _dot(f, wu_ref[...])
    act = (g * _sigmoid(g) * u).astype(BF16)
    o_ref[...] += _dot(act, wd_ref[...])

    @pl.when(kk == pl.num_programs(1) - 1)
    def _():
        y = o_ref[...]
        y = y * lax.rsqrt(jnp.mean(y * y, axis=-1, keepdims=True) + EPS) * g_ref[...]
        o_ref[...] = x_ref[...] + y


def ffn(f, w_gate, w_up, w_down, x1, g_post, tm, tf):
    m, d = f.shape
    dff = w_gate.shape[1]
    return pl.pallas_call(
        _ffn_kernel,
        grid=(m // tm, dff // tf),
        in_specs=[pl.BlockSpec((tm, d), lambda i, k: (i, 0)),
                  pl.BlockSpec((d, tf), lambda i, k: (0, k)),
                  pl.BlockSpec((d, tf), lambda i, k: (0, k)),
                  pl.BlockSpec((tf, d), lambda i, k: (k, 0)),
                  pl.BlockSpec((tm, d), lambda i, k: (i, 0)),
                  pl.BlockSpec((1, d), lambda i, k: (0, 0))],
        out_specs=pl.BlockSpec((tm, d), lambda i, k: (i, 0)),
        out_shape=jax.ShapeDtypeStruct((m, d), F32),
        compiler_params=_params("parallel", "arbitrary"),
        name="ffn",
    )(f, w_gate, w_up, w_down, x1, g_post.reshape(1, d))


def _rope_tables(pos, half):
    inv = ROPE_THETA ** (-jnp.arange(half, dtype=F32) / half)
    ang = pos.astype(F32)[:, None] * inv[None, :]
    return jnp.cos(ang), jnp.sin(ang)


def _tile_rows(table, rows):
    reps = max(1, rows // table.shape[0])
    return jnp.tile(table, (reps, 1))


def _prep_weights(w_in, w_uq, w_uk, w_uv, w_out, w_gate, w_up, w_down):
    offs = [0]
    for sz in IN_SIZES:
        offs.append(offs[-1] + sz)
    col = lambda a, b: w_in[:, offs[a]:offs[b]]
    k_pe = col(2, 3)
    half = QK_ROPE // 2
    k_pe_swap = jnp.concatenate([k_pe[:, half:], k_pe[:, :half]], axis=1)
    zpad = jnp.zeros((w_in.shape[0], LANES - QK_ROPE), w_in.dtype)
    w_lat = jnp.concatenate([col(0, 2), k_pe, zpad, k_pe_swap, zpad], axis=1)
    q_nope = w_uq[:, :, :QK_NOPE].reshape(Q_LORA, MLA_HEADS * QK_NOPE)
    q_rope = w_uq[:, :, QK_NOPE:]
    q_rope_swap = jnp.concatenate([q_rope[:, :, half:], q_rope[:, :, :half]], axis=2)
    w_q = jnp.concatenate([q_nope, q_rope.reshape(Q_LORA, -1), q_rope_swap.reshape(Q_LORA, -1)], axis=1)
    return {
        'w_qk': col(3, 5).astype(BF16),
        'w_v': col(5, 6).astype(BF16),
        'w_gates': col(6, 9).astype(BF16),
        'w_lat': w_lat.astype(BF16),
        'w_q': w_q.astype(BF16),
        'w_uk_t': jnp.swapaxes(w_uk, 1, 2).astype(BF16),
        'w_uv': w_uv.astype(BF16),
        'w_out': w_out.astype(BF16),
        'w_gate': w_gate.astype(BF16),
        'w_up': w_up.astype(BF16),
        'w_down': w_down.astype(BF16),
    }


def _row_tile(m, cap):
    t = min(m, cap)
    assert m % t == 0
    return t


def _layer(x, pos, attend, s0, batch, chunk, narrow_dtype, w, norms):
    m = x.shape[0]
    t_len = m // batch
    tm = _row_tile(m, 1024)
    tm_small = _row_tile(m, 512)

    cos_r, sin_r = _rope_tables(pos, RET_DK // 2)
    cos_r, sin_r = _tile_rows(cos_r, tm), _tile_rows(sin_r, tm)
    cos_m, sin_m = _rope_tables(pos, QK_ROPE // 2)
    reps = LANES // (QK_ROPE // 2)
    cos_m = _tile_rows(jnp.tile(cos_m, (1, reps)), tm_small)
    sin_m = _tile_rows(jnp.tile(jnp.concatenate([-sin_m, sin_m], axis=1), (1, reps // 2)), tm_small)

    h = rmsnorm_rows(x, norms['norm_mix_pre'], tm_small)
    qk = matmul_rope(h, w['w_qk'], cos_r, sin_r, narrow_dtype, tm, 512)
    v = matmul(h, w['w_v'], narrow_dtype, tm, 512)
    gates = matmul(h, w['w_gates'], narrow_dtype, tm, 512)
    cq, ckv, kpe, kcat = latent_proj(h, w['w_lat'], norms['q_norm'], norms['kv_norm'],
                                     cos_m, sin_m, tm_small)
    q_hm = query_proj(cq, w['w_q'], w['w_uk_t'], cos_m, sin_m, narrow_dtype, tm_small)
    o_lat = attend(q_hm, kcat, ckv, kpe)
    o_mla = value_up_proj(o_lat, w['w_uv'], narrow_dtype, _row_tile(m, 256))
    mixed, s_new = retention_mix(qk, v, gates, o_mla, norms['ret_norm'], s0,
                                 batch, t_len // chunk, chunk, narrow_dtype)
    x1, f = out_proj(mixed, w['w_out'], x, norms['norm_mix_post'], norms['norm_ffn_pre'],
                     tm_small)
    y = ffn(f, w['w_gate'], w['w_up'], w['w_down'], x1, norms['norm_ffn_post'], tm_small, 512)
    return y, ckv, kpe, s_new


def kernel(x_prompt, x_sample, cache_kv_latent, cache_k_rope, page_table, state_retention,
           w_in, q_norm, kv_norm, w_uq, w_uk, w_uv, ret_norm, w_out,
           norm_mix_pre, norm_mix_post, norm_ffn_pre, norm_ffn_post, w_gate, w_up, w_down):
    b, s_len, d = x_prompt.shape
    db, t_s, _ = x_sample.shape
    depth = w_in.shape[0]
    past_len = page_table.shape[1] * PAGE_SIZE
    pos_p = jnp.arange(s_len, dtype=jnp.int32)
    pos_s = past_len + jnp.arange(t_s, dtype=jnp.int32)
    pages_per_step = math.gcd(page_table.shape[1], 8)
    group = math.gcd(db, 4)

    yp = x_prompt.reshape(b * s_len, d)
    ys = x_sample.reshape(db * t_s, d)
    outs = [[] for _ in range(6)]
    for l in range(depth):
        w = _prep_weights(w_in[l], w_uq[l], w_uk[l], w_uv[l], w_out[l], w_gate[l], w_up[l], w_down[l])
        norms = {'q_norm': q_norm[l], 'kv_norm': kv_norm[l], 'ret_norm': ret_norm[l],
                 'norm_mix_pre': norm_mix_pre[l], 'norm_mix_post': norm_mix_post[l],
                 'norm_ffn_pre': norm_ffn_pre[l], 'norm_ffn_post': norm_ffn_post[l]}

        def attend_p(q_hm, kcat, ckv, kpe):
            return prompt_attention(q_hm, kcat, b, s_len, min(256, s_len), min(512, s_len))

        def attend_s(q_hm, kcat, ckv, kpe, l=l):
            cache_kpe_t = jnp.swapaxes(cache_k_rope[l], 1, 2)
            return sample_attention(q_hm, cache_kv_latent[l], cache_kpe_t, page_table,
                                    ckv, kpe, t_s, group, pages_per_step)

        yp, c1, k1, s1 = _layer(yp, pos_p, attend_p, None, b, min(RET_CHUNK, s_len), BF16, w, norms)
        ys, c2, k2, s2 = _layer(ys, pos_s, attend_s, state_retention[l], db, t_s, F32, w, norms)
        outs[0].append(c1.reshape(b, s_len, KV_LORA))
        outs[1].append(k1.reshape(b, s_len, QK_ROPE))
        outs[2].append(s1)
        outs[3].append(c2.reshape(db, t_s, KV_LORA))
        outs[4].append(k2.reshape(db, t_s, QK_ROPE))
        outs[5].append(s2)
    return (yp.reshape(b, s_len, d), ys.reshape(db, t_s, d)) + tuple(jnp.stack(o) for o in outs)
```

```python
import functools
import math

import jax
import jax.numpy as jnp
from jax import lax
from jax.experimental import pallas as pl
from jax.experimental.pallas import tpu as pltpu

F32 = jnp.float32
BF16 = jnp.bfloat16

D_MODEL = 2048
PAGE_SIZE = 128
MLA_HEADS = 16
Q_LORA = 512
KV_LORA = 512
QK_NOPE = 128
QK_ROPE = 64
V_HEAD = 128
MLA_WIDTH = MLA_HEADS * V_HEAD
MLA_SCALE = (QK_NOPE + QK_ROPE) ** -0.5
SCORE_SCALE_LOG2 = MLA_SCALE * math.log2(math.e)
K_CAT = KV_LORA + QK_ROPE
RET_HEADS = 8
RET_DK = D_MODEL // RET_HEADS
RET_DV = D_MODEL // RET_HEADS
RET_WIDTH = RET_HEADS * RET_DV
RET_K_SCALE = RET_DK ** -0.5
RET_CHUNK = 128
ROPE_THETA = 10000.0
EPS = 1e-6
NEG_INF = -1e30
IN_SIZES = (Q_LORA, KV_LORA, QK_ROPE, RET_HEADS * RET_DK, RET_HEADS * RET_DK,
            RET_WIDTH, RET_WIDTH, MLA_WIDTH, RET_WIDTH)

LANES = 128
VMEM_LIMIT = 56 * 1024 * 1024


def _params(*sem):
    return pltpu.CompilerParams(dimension_semantics=sem, vmem_limit_bytes=VMEM_LIMIT)


def _sigmoid(x):
    return 0.5 * jnp.tanh(0.5 * x) + 0.5


def _dot(a, b):
    return jnp.dot(a, b, preferred_element_type=F32)


def _dot_nt(a, b):
    return lax.dot_general(a, b, (((1,), (1,)), ((), ())), preferred_element_type=F32)


def _dot_tn(a, b):
    return lax.dot_general(a, b, (((0,), (0,)), ((), ())), preferred_element_type=F32)


def _rmsnorm_kernel(x_ref, g_ref, o_ref):
    x = x_ref[...]
    ms = jnp.mean(x * x, axis=-1, keepdims=True)
    o_ref[...] = (x * lax.rsqrt(ms + EPS) * g_ref[...]).astype(o_ref.dtype)


def rmsnorm_rows(x, g, tm):
    m, d = x.shape
    return pl.pallas_call(
        _rmsnorm_kernel,
        grid=(m // tm,),
        in_specs=[pl.BlockSpec((tm, d), lambda i: (i, 0)),
                  pl.BlockSpec((1, d), lambda i: (0, 0))],
        out_specs=pl.BlockSpec((tm, d), lambda i: (i, 0)),
        out_shape=jax.ShapeDtypeStruct((m, d), BF16),
        compiler_params=_params("parallel"),
        name="rmsnorm_rows",
    )(x, g.reshape(1, d))


def _mm_kernel(a_ref, w_ref, o_ref):
    o_ref[...] = _dot(a_ref[...], w_ref[...]).astype(o_ref.dtype)


def matmul(a, w, out_dtype, tm, tn):
    m, k = a.shape
    n = w.shape[1]
    return pl.pallas_call(
        _mm_kernel,
        grid=(m // tm, n // tn),
        in_specs=[pl.BlockSpec((tm, k), lambda i, j: (i, 0)),
                  pl.BlockSpec((k, tn), lambda i, j: (0, j))],
        out_specs=pl.BlockSpec((tm, tn), lambda i, j: (i, j)),
        out_shape=jax.ShapeDtypeStruct((m, n), out_dtype),
        compiler_params=_params("parallel", "parallel"),
        name="matmul",
    )(a, w)


def _mm_rope_kernel(a_ref, w_ref, cos_ref, sin_ref, o_ref, *, n_q_tiles, heads_per_tile):
    acc = _dot(a_ref[...], w_ref[...])
    c = cos_ref[...]
    s = sin_ref[...]
    scale = jnp.where(pl.program_id(1) >= n_q_tiles, RET_K_SCALE, 1.0).astype(F32)
    half = RET_DK // 2
    for h in range(heads_per_tile):
        lo = h * RET_DK
        x1 = acc[:, lo:lo + half]
        x2 = acc[:, lo + half:lo + RET_DK]
        o_ref[:, lo:lo + half] = ((x1 * c - x2 * s) * scale).astype(o_ref.dtype)
        o_ref[:, lo + half:lo + RET_DK] = ((x1 * s + x2 * c) * scale).astype(o_ref.dtype)


def matmul_rope(a, w, cos, sin, out_dtype, tm, tn):
    m, k = a.shape
    n = w.shape[1]
    pos_blocks = cos.shape[0] // tm
    kern = functools.partial(_mm_rope_kernel, n_q_tiles=(n // 2) // tn,
                             heads_per_tile=tn // RET_DK)
    return pl.pallas_call(
        kern,
        grid=(m // tm, n // tn),
        in_specs=[pl.BlockSpec((tm, k), lambda i, j: (i, 0)),
                  pl.BlockSpec((k, tn), lambda i, j: (0, j)),
                  pl.BlockSpec((tm, RET_DK // 2), lambda i, j: (i % pos_blocks, 0)),
                  pl.BlockSpec((tm, RET_DK // 2), lambda i, j: (i % pos_blocks, 0))],
        out_specs=pl.BlockSpec((tm, tn), lambda i, j: (i, j)),
        out_shape=jax.ShapeDtypeStruct((m, n), out_dtype),
        compiler_params=_params("parallel", "parallel"),
        name="matmul_rope",
    )(a, w, cos, sin)


def _latent_kernel(a_ref, w_ref, qn_ref, kvn_ref, cos_ref, sin_ref,
                   cq_ref, ckv_ref, kpe_ref, kcat_ref):
    acc = _dot(a_ref[...], w_ref[...])
    cq = acc[:, :Q_LORA]
    cq = cq * lax.rsqrt(jnp.mean(cq * cq, axis=-1, keepdims=True) + EPS) * qn_ref[...]
    cq_ref[...] = cq.astype(cq_ref.dtype)
    ckv = acc[:, Q_LORA:Q_LORA + KV_LORA]
    ckv = ckv * lax.rsqrt(jnp.mean(ckv * ckv, axis=-1, keepdims=True) + EPS) * kvn_ref[...]
    ckv_ref[...] = ckv
    base = Q_LORA + KV_LORA
    x = acc[:, base:base + LANES]
    xs = acc[:, base + LANES:base + 2 * LANES]
    kpe = (x * cos_ref[...] + xs * sin_ref[...])[:, :QK_ROPE]
    kpe_ref[...] = kpe
    kcat_ref[:, :KV_LORA] = ckv.astype(kcat_ref.dtype)
    kcat_ref[:, KV_LORA:K_CAT] = kpe.astype(kcat_ref.dtype)


def latent_proj(a, w, q_norm, kv_norm, cos, sin, tm):
    m, k = a.shape
    n = w.shape[1]
    pos_blocks = cos.shape[0] // tm
    row = lambda i: (i, 0)
    return pl.pallas_call(
        _latent_kernel,
        grid=(m // tm,),
        in_specs=[pl.BlockSpec((tm, k), row),
                  pl.BlockSpec((k, n), lambda i: (0, 0)),
                  pl.BlockSpec((1, Q_LORA), lambda i: (0, 0)),
                  pl.BlockSpec((1, KV_LORA), lambda i: (0, 0)),
                  pl.BlockSpec((tm, LANES), lambda i: (i % pos_blocks, 0)),
                  pl.BlockSpec((tm, LANES), lambda i: (i % pos_blocks, 0))],
        out_specs=[pl.BlockSpec((tm, Q_LORA), row),
                   pl.BlockSpec((tm, KV_LORA), row),
                   pl.BlockSpec((tm, QK_ROPE), row),
                   pl.BlockSpec((tm, K_CAT), row)],
        out_shape=[jax.ShapeDtypeStruct((m, Q_LORA), BF16),
                   jax.ShapeDtypeStruct((m, KV_LORA), F32),
                   jax.ShapeDtypeStruct((m, QK_ROPE), F32),
                   jax.ShapeDtypeStruct((m, K_CAT), BF16)],
        compiler_params=_params("parallel"),
        name="latent_proj",
    )(a, w, q_norm.reshape(1, Q_LORA), kv_norm.reshape(1, KV_LORA), cos, sin)


def _query_kernel(cq_ref, wq_ref, wuk_ref, cos_ref, sin_ref, o_ref):
    q = _dot(cq_ref[...], wq_ref[...])
    for h in range(MLA_HEADS):
        qn = q[:, h * QK_NOPE:(h + 1) * QK_NOPE].astype(BF16)
        o_ref[h, :, :KV_LORA] = (_dot(qn, wuk_ref[h]) * SCORE_SCALE_LOG2).astype(o_ref.dtype)
    c = cos_ref[...] * SCORE_SCALE_LOG2
    s = sin_ref[...] * SCORE_SCALE_LOG2
    rope0 = MLA_HEADS * QK_NOPE
    swap0 = rope0 + MLA_HEADS * QK_ROPE
    for p in range(MLA_HEADS * QK_ROPE // LANES):
        x = q[:, rope0 + p * LANES:rope0 + (p + 1) * LANES]
        xs = q[:, swap0 + p * LANES:swap0 + (p + 1) * LANES]
        qp = x * c + xs * s
        o_ref[2 * p, :, KV_LORA:K_CAT] = qp[:, :QK_ROPE].astype(o_ref.dtype)
        o_ref[2 * p + 1, :, KV_LORA:K_CAT] = qp[:, QK_ROPE:].astype(o_ref.dtype)


def query_proj(cq, wq, wuk_t, cos, sin, out_dtype, tm):
    m = cq.shape[0]
    pos_blocks = cos.shape[0] // tm
    return pl.pallas_call(
        _query_kernel,
        grid=(m // tm,),
        in_specs=[pl.BlockSpec((tm, Q_LORA), lambda i: (i, 0)),
                  pl.BlockSpec(wq.shape, lambda i: (0, 0)),
                  pl.BlockSpec(wuk_t.shape, lambda i: (0, 0, 0)),
                  pl.BlockSpec((tm, LANES), lambda i: (i % pos_blocks, 0)),
                  pl.BlockSpec((tm, LANES), lambda i: (i % pos_blocks, 0))],
        out_specs=pl.BlockSpec((MLA_HEADS, tm, K_CAT), lambda i: (0, i, 0)),
        out_shape=jax.ShapeDtypeStruct((MLA_HEADS, m, K_CAT), out_dtype),
        compiler_params=_params("parallel"),
        name="query_proj",
    )(cq, wq, wuk_t, cos, sin)


def _lane_partial_sum(p):
    out = p[:, :LANES]
    for c in range(1, p.shape[1] // LANES):
        out = out + p[:, c * LANES:(c + 1) * LANES]
    return out


def _softmax_update(s, v, m_prev, l_prev, acc_prev):
    m_new = jnp.maximum(m_prev, jnp.max(s, axis=-1, keepdims=True))
    corr = jnp.exp2(m_prev - m_new)
    p = jnp.exp2(s - m_new)
    l_new = l_prev * corr + _lane_partial_sum(p)
    acc_new = acc_prev * corr + _dot(p.astype(BF16), v)
    return m_new, l_new, acc_new


def _softmax_finish(l, acc):
    return acc / jnp.sum(l, axis=-1, keepdims=True)


PAIR_FIRST, PAIR_LAST = 1, 2
ATTN_GROUP_ROWS = 256


def _prompt_attn_kernel(it_ref, jt_ref, fl_ref, q_ref, k_ref, o_ref, m_ref, l_ref, acc_ref,
                        *, tq, tk):
    p = pl.program_id(1)
    i = it_ref[p]
    j = jt_ref[p]
    flags = fl_ref[p]
    hg = max(1, ATTN_GROUP_ROWS // tq)
    n_groups = MLA_HEADS // hg
    rows = hg * tq

    @pl.when((flags & PAIR_FIRST) != 0)
    def _():
        m_ref[...] = jnp.full(m_ref.shape, NEG_INF, F32)
        l_ref[...] = jnp.zeros(l_ref.shape, F32)
        acc_ref[...] = jnp.zeros(acc_ref.shape, F32)

    def step(masked, nk=tk):
        k = k_ref[:nk]
        v = k[:, :KV_LORA]
        if masked:
            qpos = i * tq + lax.broadcasted_iota(jnp.int32, (tq, nk), 0)
            kpos = j * tk + lax.broadcasted_iota(jnp.int32, (tq, nk), 1)
            keep = (kpos <= qpos)[None]
        def scores(g):
            q = q_ref[g * hg:(g + 1) * hg].reshape(rows, K_CAT)
            s = _dot_nt(q, k)
            if masked:
                s = jnp.where(keep, s.reshape(hg, tq, nk), NEG_INF).reshape(rows, nk)
            return s

        s_next = scores(0)
        for g in range(n_groups):
            rs = slice(g * rows, (g + 1) * rows)
            s = s_next
            if g + 1 < n_groups:
                s_next = scores(g + 1)
            m_new, l_new, acc_new = _softmax_update(s, v, m_ref[rs], l_ref[rs], acc_ref[rs])
            m_ref[rs] = m_new
            l_ref[rs] = l_new
            acc_ref[rs] = acc_new

    variant = flags >> 2

    @pl.when(variant == 0)
    def _():
        step(False)

    for n in range(1, tk // tq + 1):
        @pl.when(variant == n)
        def _(n=n):
            step(True, n * tq)

    @pl.when((flags & PAIR_LAST) != 0)
    def _():
        o = _softmax_finish(l_ref[...], acc_ref[...])
        o_ref[...] = o.reshape(MLA_HEADS, tq, KV_LORA).astype(o_ref.dtype)


def prompt_attention(q_hm, kcat, batch, seq, tq, tk):
    nq = seq // tq
    nk = seq // tk
    rows = MLA_HEADS * tq
    assert tk % tq == 0
    it, jt, fl = [], [], []
    for i in range(nq):
        last_j = (i * tq + tq - 1) // tk
        for j in range(last_j + 1):
            masked = j * tk + tk - 1 > i * tq
            variant = (i * tq + tq - j * tk) // tq if masked else 0
            it.append(i)
            jt.append(j)
            fl.append((PAIR_FIRST if j == 0 else 0) | (PAIR_LAST if j == last_j else 0)
                      | (variant << 2))
    tables = [jnp.asarray(t, jnp.int32) for t in (it, jt, fl)]
    q_map = lambda b, p, it, jt, fl: (0, b * nq + it[p], 0)
    kern = functools.partial(_prompt_attn_kernel, tq=tq, tk=tk)
    grid_spec = pltpu.PrefetchScalarGridSpec(
        num_scalar_prefetch=3,
        grid=(batch, len(it)),
        in_specs=[pl.BlockSpec((MLA_HEADS, tq, K_CAT), q_map),
                  pl.BlockSpec((tk, K_CAT), lambda b, p, it, jt, fl: (b * nk + jt[p], 0))],
        out_specs=pl.BlockSpec((MLA_HEADS, tq, KV_LORA), q_map),
        scratch_shapes=[pltpu.VMEM((rows, 1), F32),
                        pltpu.VMEM((rows, LANES), F32),
                        pltpu.VMEM((rows, KV_LORA), F32)],
    )
    return pl.pallas_call(
        kern,
        grid_spec=grid_spec,
        out_shape=jax.ShapeDtypeStruct((MLA_HEADS, batch * seq, KV_LORA), BF16),
        compiler_params=_params("parallel", "arbitrary"),
        name="prompt_attention",
    )(*tables, q_hm, kcat)


RING_SLOTS = 2


def _sample_attn_kernel(pt_ref, q_ref, ckv_hbm, kpe_hbm, ckv_new_ref, kpe_new_ref, o_ref,
                        cring_ref, rring_ref, sem, cbuf_ref, rbuf_ref, knew_ref,
                        m_ref, l_ref, acc_ref, *, group, pages_per_step, t_new):
    j = pl.program_id(1)
    n_j = pl.num_programs(1)
    step = pl.program_id(0) * n_j + j
    n_steps = pl.num_programs(0) * n_j
    slot = lax.rem(step, RING_SLOTS)
    rows = MLA_HEADS * t_new

    def page_copies(at_step, at_slot, g):
        b_at = lax.div(at_step, n_j)
        j_at = lax.rem(at_step, n_j)
        out = []
        for u in range(pages_per_step):
            page = pt_ref[b_at * group + g, j_at * pages_per_step + u]
            idx = g * pages_per_step + u
            out.append(pltpu.make_async_copy(ckv_hbm.at[page], cring_ref.at[at_slot, idx],
                                             sem.at[at_slot, g]))
            out.append(pltpu.make_async_copy(kpe_hbm.at[page], rring_ref.at[at_slot, idx],
                                             sem.at[at_slot, g]))
        return out

    def start_all(copies):
        for n, c in enumerate(copies):
            c.start(priority=(n // 2 + n) % 2)

    @pl.when(step == 0)
    def _():
        for g in range(group):
            start_all(page_copies(step, slot, g))

    @pl.when(j == 0)
    def _():
        m_ref[...] = jnp.full(m_ref.shape, NEG_INF, F32)
        l_ref[...] = jnp.zeros(l_ref.shape, F32)
        acc_ref[...] = jnp.zeros(acc_ref.shape, F32)

    def query(g):
        return q_ref[:, g * t_new:(g + 1) * t_new, :].reshape(rows, K_CAT).astype(BF16)

    def scores(g):
        @pl.when(step + 1 < n_steps)
        def _():
            start_all(page_copies(step + 1, RING_SLOTS - 1 - slot, g))

        for c in page_copies(step, slot, g):
            c.wait()
        for u in range(pages_per_step):
            ps = slice(u * PAGE_SIZE, (u + 1) * PAGE_SIZE)
            idx = g * pages_per_step + u
            cbuf_ref[g, ps, :] = cring_ref[slot, idx].astype(BF16)
            rbuf_ref[g, :, ps] = rring_ref[slot, idx].astype(BF16)
        q = query(g)
        return _dot_nt(q[:, :KV_LORA], cbuf_ref[g]) + _dot(q[:, KV_LORA:], rbuf_ref[g])

    s_next = scores(0)
    for g in range(group):
        s = s_next
        if g + 1 < group:
            s_next = scores(g + 1)
        m_new, l_new, acc_new = _softmax_update(s, cbuf_ref[g], m_ref[g], l_ref[g], acc_ref[g])
        m_ref[g] = m_new
        l_ref[g] = l_new
        acc_ref[g] = acc_new

    @pl.when(j == pl.num_programs(1) - 1)
    def _():
        qpos = lax.broadcasted_iota(jnp.int32, (t_new, PAGE_SIZE), 0)
        kpos = lax.broadcasted_iota(jnp.int32, (t_new, PAGE_SIZE), 1)
        keep = (kpos <= qpos)[None]
        for g in range(group):
            ts = slice(g * t_new, (g + 1) * t_new)
            knew_ref[...] = jnp.zeros(knew_ref.shape, F32)
            knew_ref[:t_new, :KV_LORA] = ckv_new_ref[ts, :]
            knew_ref[:t_new, KV_LORA:K_CAT] = kpe_new_ref[ts, :]
            kn = knew_ref[...].astype(BF16)
            sn = _dot_nt(query(g), kn)
            sn = jnp.where(keep, sn.reshape(MLA_HEADS, t_new, PAGE_SIZE), NEG_INF)
            _, l_new, acc_new = _softmax_update(sn.reshape(rows, PAGE_SIZE), kn[:, :KV_LORA],
                                                m_ref[g], l_ref[g], acc_ref[g])
            o = _softmax_finish(l_new, acc_new)
            o_ref[:, ts, :] = o.reshape(MLA_HEADS, t_new, KV_LORA).astype(o_ref.dtype)


def sample_attention(q_hm, cache_ckv, cache_kpe_t, page_table, ckv_new, kpe_new, t_new,
                     group, pages_per_step):
    dec_batch, n_pages = page_table.shape
    n_steps = n_pages // pages_per_step
    rows = MLA_HEADS * t_new
    gt = group * t_new

    in_specs = [pl.BlockSpec((MLA_HEADS, gt, K_CAT), lambda b, j, pt: (0, b, 0)),
                pl.BlockSpec(memory_space=pl.ANY),
                pl.BlockSpec(memory_space=pl.ANY),
                pl.BlockSpec((gt, KV_LORA), lambda b, j, pt: (b, 0)),
                pl.BlockSpec((gt, QK_ROPE), lambda b, j, pt: (b, 0))]
    kern = functools.partial(_sample_attn_kernel, group=group,
                             pages_per_step=pages_per_step, t_new=t_new)
    keys = pages_per_step * PAGE_SIZE
    n_slots = group * pages_per_step
    grid_spec = pltpu.PrefetchScalarGridSpec(
        num_scalar_prefetch=1,
        grid=(dec_batch // group, n_steps),
        in_specs=in_specs,
        out_specs=pl.BlockSpec((MLA_HEADS, gt, KV_LORA), lambda b, j, pt: (0, b, 0)),
        scratch_shapes=[pltpu.VMEM((RING_SLOTS, n_slots, PAGE_SIZE, KV_LORA), F32),
                        pltpu.VMEM((RING_SLOTS, n_slots, QK_ROPE, PAGE_SIZE), F32),
                        pltpu.SemaphoreType.DMA((RING_SLOTS, group)),
                        pltpu.VMEM((group, keys, KV_LORA), BF16),
                        pltpu.VMEM((group, QK_ROPE, keys), BF16),
                        pltpu.VMEM((PAGE_SIZE, K_CAT), F32),
                        pltpu.VMEM((group, rows, 1), F32),
                        pltpu.VMEM((group, rows, LANES), F32),
                        pltpu.VMEM((group, rows, KV_LORA), F32)],
    )
    return pl.pallas_call(
        kern,
        grid_spec=grid_spec,
        out_shape=jax.ShapeDtypeStruct((MLA_HEADS, dec_batch * t_new, KV_LORA), F32),
        compiler_params=_params("arbitrary", "arbitrary"),
        name="sample_attention",
    )(page_table, q_hm, cache_ckv, cache_kpe_t, ckv_new, kpe_new)


def _uv_kernel(o_ref, w_ref, out_ref):
    for h in range(MLA_HEADS):
        out_ref[:, h * V_HEAD:(h + 1) * V_HEAD] = _dot(o_ref[h].astype(BF16),
                                                       w_ref[h]).astype(out_ref.dtype)


def value_up_proj(o_lat, w_uv, out_dtype, tm):
    heads, m, _ = o_lat.shape
    return pl.pallas_call(
        _uv_kernel,
        grid=(m // tm,),
        in_specs=[pl.BlockSpec((heads, tm, KV_LORA), lambda i: (0, i, 0)),
                  pl.BlockSpec((heads, KV_LORA, V_HEAD), lambda i: (0, 0, 0))],
        out_specs=pl.BlockSpec((tm, heads * V_HEAD), lambda i: (i, 0)),
        out_shape=jax.ShapeDtypeStruct((m, heads * V_HEAD), out_dtype),
        compiler_params=_params("parallel"),
        name="value_up_proj",
    )(o_lat, w_uv)


def _retention_kernel(*refs, chunk, has_s0):
    if has_s0:
        (q_ref, k_ref, v_ref, rg_ref, gm_ref, gr_ref, om_ref, rn_ref, s0_ref,
         mix_ref, sout_ref, s_ref) = refs
    else:
        (q_ref, k_ref, v_ref, rg_ref, gm_ref, gr_ref, om_ref, rn_ref,
         mix_ref, sout_ref, s_ref) = refs
    c = pl.program_id(1)

    @pl.when(c == 0)
    def _():
        if has_s0:
            s_ref[...] = s0_ref[0]
        else:
            s_ref[...] = jnp.zeros(s_ref.shape, F32)

    ii = lax.broadcasted_iota(jnp.int32, (chunk, chunk), 0)
    jj = lax.broadcasted_iota(jnp.int32, (chunk, chunk), 1)
    diff = (ii - jj).astype(F32)
    idx = lax.broadcasted_iota(jnp.int32, (chunk, 1), 0).astype(F32)
    for h in range(RET_HEADS):
        lg = math.log(1.0 - 2.0 ** (-5.0 - h))
        dmat = jnp.where(diff >= 0, jnp.exp(jnp.maximum(diff, 0.0) * lg), 0.0)
        q_dec = jnp.exp((idx + 1.0) * lg)
        k_dec = jnp.exp((chunk - 1.0 - idx) * lg)
        c_dec = math.exp(chunk * lg)
        sl = slice(h * RET_DK, (h + 1) * RET_DK)
        q = q_ref[:, sl].astype(BF16)
        k = k_ref[:, sl]
        v = v_ref[:, sl].astype(BF16)
        state = s_ref[h]
        a = _dot_nt(q, k.astype(BF16)) * dmat
        o = _dot(a.astype(BF16), v) + _dot(q, state.astype(BF16)) * q_dec
        kd = (k.astype(F32) * k_dec).astype(BF16)
        s_ref[h] = state * c_dec + _dot_tn(kd, v)
        mu = jnp.mean(o, axis=-1, keepdims=True)
        d = o - mu
        var = jnp.mean(d * d, axis=-1, keepdims=True)
        on = d * lax.rsqrt(var + EPS) * rn_ref[:, sl]
        rg = rg_ref[:, sl].astype(F32)
        o_ret = rg * _sigmoid(rg) * on
        mixed = (_sigmoid(gm_ref[:, sl].astype(F32)) * om_ref[:, sl].astype(F32)
                 + _sigmoid(gr_ref[:, sl].astype(F32)) * o_ret)
        mix_ref[:, sl] = mixed.astype(mix_ref.dtype)

    @pl.when(c == pl.num_programs(1) - 1)
    def _():
        sout_ref[0] = s_ref[...]


def retention_mix(qk, v, gates, o_mla, ret_norm, s0, batch, n_chunks, chunk, mix_dtype):
    m = v.shape[0]
    has_s0 = s0 is not None
    row = lambda col: (lambda b, c: (b * n_chunks + c, col))
    blk = lambda col: pl.BlockSpec((chunk, RET_WIDTH), row(col))
    state_spec = pl.BlockSpec((1, RET_HEADS, RET_DK, RET_DV), lambda b, c: (b, 0, 0, 0))
    in_specs = [blk(0), blk(1), blk(0), blk(0), blk(1), blk(2), blk(0),
                pl.BlockSpec((1, RET_WIDTH), lambda b, c: (0, 0))]
    args = [qk, qk, v, gates, gates, gates, o_mla, ret_norm.reshape(1, RET_WIDTH)]
    if has_s0:
        in_specs.append(state_spec)
        args.append(s0)
    kern = functools.partial(_retention_kernel, chunk=chunk, has_s0=has_s0)
    return pl.pallas_call(
        kern,
        grid=(batch, n_chunks),
        in_specs=in_specs,
        out_specs=[blk(0), state_spec],
        out_shape=[jax.ShapeDtypeStruct((m, RET_WIDTH), mix_dtype),
                   jax.ShapeDtypeStruct((batch, RET_HEADS, RET_DK, RET_DV), F32)],
        scratch_shapes=[pltpu.VMEM((RET_HEADS, RET_DK, RET_DV), F32)],
        compiler_params=_params("parallel", "arbitrary"),
        name="retention_mix",
    )(*args)


def _out_proj_kernel(mix_ref, w_ref, x_ref, gpost_ref, gpre_ref, x1_ref, f_ref):
    y = _dot(mix_ref[...].astype(BF16), w_ref[...])
    y = y * lax.rsqrt(jnp.mean(y * y, axis=-1, keepdims=True) + EPS) * gpost_ref[...]
    x1 = x_ref[...] + y
    x1_ref[...] = x1
    f = x1 * lax.rsqrt(jnp.mean(x1 * x1, axis=-1, keepdims=True) + EPS) * gpre_ref[...]
    f_ref[...] = f.astype(f_ref.dtype)


def out_proj(mixed, w_out, x, g_post, g_pre, tm):
    m, d = x.shape
    row = lambda i: (i, 0)
    vec = pl.BlockSpec((1, d), lambda i: (0, 0))
    return pl.pallas_call(
        _out_proj_kernel,
        grid=(m // tm,),
        in_specs=[pl.BlockSpec((tm, d), row), pl.BlockSpec((d, d), lambda i: (0, 0)),
                  pl.BlockSpec((tm, d), row), vec, vec],
        out_specs=[pl.BlockSpec((tm, d), row), pl.BlockSpec((tm, d), row)],
        out_shape=[jax.ShapeDtypeStruct((m, d), F32), jax.ShapeDtypeStruct((m, d), BF16)],
        compiler_params=_params("parallel"),
        name="out_proj",
    )(mixed, w_out, x, g_post.reshape(1, d), g_pre.reshape(1, d))


def _ffn_kernel(f_ref, wg_ref, wu_ref, wd_ref, x_ref, g_ref, o_ref):
    kk = pl.program_id(1)

    @pl.when(kk == 0)
    def _():
        o_ref[...] = jnp.zeros(o_ref.shape, F32)

    f = f_ref[...]
    g = _dot(f, wg_ref[...])
    u = _dot(f, wu_ref[...])
    act = (g * _sigmoid(g) * u).astype(BF16)
    o_ref[...] += _dot(act, wd_ref[...])

    @pl.when(kk == pl.num_programs(1) - 1)
    def _():
        y = o_ref[...]
        y = y * lax.rsqrt(jnp.mean(y * y, axis=-1, keepdims=True) + EPS) * g_ref[...]
        o_ref[...] = x_ref[...] + y


def ffn(f, w_gate, w_up, w_down, x1, g_post, tm, tf):
    m, d = f.shape
    dff = w_gate.shape[1]
    return pl.pallas_call(
        _ffn_kernel,
        grid=(m // tm, dff // tf),
        in_specs=[pl.BlockSpec((tm, d), lambda i, k: (i, 0)),
                  pl.BlockSpec((d, tf), lambda i, k: (0, k)),
                  pl.BlockSpec((d, tf), lambda i, k: (0, k)),
                  pl.BlockSpec((tf, d), lambda i, k: (k, 0)),
                  pl.BlockSpec((tm, d), lambda i, k: (i, 0)),
                  pl.BlockSpec((1, d), lambda i, k: (0, 0))],
        out_specs=pl.BlockSpec((tm, d), lambda i, k: (i, 0)),
        out_shape=jax.ShapeDtypeStruct((m, d), F32),
        compiler_params=_params("parallel", "arbitrary"),
        name="ffn",
    )(f, w_gate, w_up, w_down, x1, g_post.reshape(1, d))


def _rope_tables(pos, half):
    inv = ROPE_THETA ** (-jnp.arange(half, dtype=F32) / half)
    ang = pos.astype(F32)[:, None] * inv[None, :]
    return jnp.cos(ang), jnp.sin(ang)


def _tile_rows(table, rows):
    reps = max(1, rows // table.shape[0])
    return jnp.tile(table, (reps, 1))


def _prep_weights(w_in, w_uq, w_uk, w_uv, w_out, w_gate, w_up, w_down):
    offs = [0]
    for sz in IN_SIZES:
        offs.append(offs[-1] + sz)
    col = lambda a, b: w_in[:, offs[a]:offs[b]]
    k_pe = col(2, 3)
    half = QK_ROPE // 2
    k_pe_swap = jnp.concatenate([k_pe[:, half:], k_pe[:, :half]], axis=1)
    zpad = jnp.zeros((w_in.shape[0], LANES - QK_ROPE), w_in.dtype)
    w_lat = jnp.concatenate([col(0, 2), k_pe, zpad, k_pe_swap, zpad], axis=1)
    q_nope = w_uq[:, :, :QK_NOPE].reshape(Q_LORA, MLA_HEADS * QK_NOPE)
    q_rope = w_uq[:, :, QK_NOPE:]
    q_rope_swap = jnp.concatenate([q_rope[:, :, half:], q_rope[:, :, :half]], axis=2)
    w_q = jnp.concatenate([q_nope, q_rope.reshape(Q_LORA, -1), q_rope_swap.reshape(Q_LORA, -1)], axis=1)
    return {
        'w_qk': col(3, 5).astype(BF16),
        'w_v': col(5, 6).astype(BF16),
        'w_gates': col(6, 9).astype(BF16),
        'w_lat': w_lat.astype(BF16),
        'w_q': w_q.astype(BF16),
        'w_uk_t': jnp.swapaxes(w_uk, 1, 2).astype(BF16),
        'w_uv': w_uv.astype(BF16),
        'w_out': w_out.astype(BF16),
        'w_gate': w_gate.astype(BF16),
        'w_up': w_up.astype(BF16),
        'w_down': w_down.astype(BF16),
    }


def _row_tile(m, cap):
    t = min(m, cap)
    assert m % t == 0
    return t


def _layer(x, pos, attend, s0, batch, chunk, narrow_dtype, w, norms):
    m = x.shape[0]
    t_len = m // batch
    tm = _row_tile(m, 1024)
    tm_small = _row_tile(m, 512)

    cos_r, sin_r = _rope_tables(pos, RET_DK // 2)
    cos_r, sin_r = _tile_rows(cos_r, tm), _tile_rows(sin_r, tm)
    cos_m, sin_m = _rope_tables(pos, QK_ROPE // 2)
    reps = LANES // (QK_ROPE // 2)
    cos_m = _tile_rows(jnp.tile(cos_m, (1, reps)), tm_small)
    sin_m = _tile_rows(jnp.tile(jnp.concatenate([-sin_m, sin_m], axis=1), (1, reps // 2)), tm_small)

    h = rmsnorm_rows(x, norms['norm_mix_pre'], tm_small)
    qk = matmul_rope(h, w['w_qk'], cos_r, sin_r, narrow_dtype, tm, 512)
    v = matmul(h, w['w_v'], narrow_dtype, tm, 512)
    gates = matmul(h, w['w_gates'], narrow_dtype, tm, 512)
    cq, ckv, kpe, kcat = latent_proj(h, w['w_lat'], norms['q_norm'], norms['kv_norm'],
                                     cos_m, sin_m, tm_small)
    q_hm = query_proj(cq, w['w_q'], w['w_uk_t'], cos_m, sin_m, narrow_dtype, tm_small)
    o_lat = attend(q_hm, kcat, ckv, kpe)
    o_mla = value_up_proj(o_lat, w['w_uv'], narrow_dtype, _row_tile(m, 256))
    mixed, s_new = retention_mix(qk, v, gates, o_mla, norms['ret_norm'], s0,
                                 batch, t_len // chunk, chunk, narrow_dtype)
    x1, f = out_proj(mixed, w['w_out'], x, norms['norm_mix_post'], norms['norm_ffn_pre'],
                     tm_small)
    y = ffn(f, w['w_gate'], w['w_up'], w['w_down'], x1, norms['norm_ffn_post'], tm_small, 512)
    return y, ckv, kpe, s_new


def kernel(x_prompt, x_sample, cache_kv_latent, cache_k_rope, page_table, state_retention,
           w_in, q_norm, kv_norm, w_uq, w_uk, w_uv, ret_norm, w_out,
           norm_mix_pre, norm_mix_post, norm_ffn_pre, norm_ffn_post, w_gate, w_up, w_down):
    b, s_len, d = x_prompt.shape
    db, t_s, _ = x_sample.shape
    depth = w_in.shape[0]
    past_len = page_table.shape[1] * PAGE_SIZE
    pos_p = jnp.arange(s_len, dtype=jnp.int32)
    pos_s = past_len + jnp.arange(t_s, dtype=jnp.int32)
    pages_per_step = math.gcd(page_table.shape[1], 8)
    group = math.gcd(db, 4)

    yp = x_prompt.reshape(b * s_len, d)
    ys = x_sample.reshape(db * t_s, d)
    outs = [[] for _ in range(6)]
    for l in range(depth):
        w = _prep_weights(w_in[l], w_uq[l], w_uk[l], w_uv[l], w_out[l], w_gate[l], w_up[l], w_down[l])
        norms = {'q_norm': q_norm[l], 'kv_norm': kv_norm[l], 'ret_norm': ret_norm[l],
                 'norm_mix_pre': norm_mix_pre[l], 'norm_mix_post': norm_mix_post[l],
                 'norm_ffn_pre': norm_ffn_pre[l], 'norm_ffn_post': norm_ffn_post[l]}

        def attend_p(q_hm, kcat, ckv, kpe):
            return prompt_attention(q_hm, kcat, b, s_len, min(256, s_len), min(1024, s_len))

        def attend_s(q_hm, kcat, ckv, kpe, l=l):
            cache_kpe_t = jnp.swapaxes(cache_k_rope[l], 1, 2)
            return sample_attention(q_hm, cache_kv_latent[l], cache_kpe_t, page_table,
                                    ckv, kpe, t_s, group, pages_per_step)

        yp, c1, k1, s1 = _layer(yp, pos_p, attend_p, None, b, min(RET_CHUNK, s_len), BF16, w, norms)
        ys, c2, k2, s2 = _layer(ys, pos_s, attend_s, state_retention[l], db, t_s, F32, w, norms)
        outs[0].append(c1.reshape(b, s_len, KV_LORA))
        outs[1].append(k1.reshape(b, s_len, QK_ROPE))
        outs[2].append(s1)
        outs[3].append(c2.reshape(db, t_s, KV_LORA))
        outs[4].append(k2.reshape(db, t_s, QK_ROPE))
        outs[5].append(s2)
    return (yp.reshape(b, s_len, d), ys.reshape(db, t_s, d)) + tuple(jnp.stack(o) for o in outs)
```

```python
import functools
import math

import jax
import jax.numpy as jnp
from jax import lax
from jax.experimental import pallas as pl
from jax.experimental.pallas import tpu as pltpu

F32 = jnp.float32
BF16 = jnp.bfloat16

D_MODEL = 2048
PAGE_SIZE = 128
MLA_HEADS = 16
Q_LORA = 512
KV_LORA = 512
QK_NOPE = 128
QK_ROPE = 64
V_HEAD = 128
MLA_WIDTH = MLA_HEADS * V_HEAD
MLA_SCALE = (QK_NOPE + QK_ROPE) ** -0.5
SCORE_SCALE_LOG2 = MLA_SCALE * math.log2(math.e)
K_CAT = KV_LORA + QK_ROPE
RET_HEADS = 8
RET_DK = D_MODEL // RET_HEADS
RET_DV = D_MODEL // RET_HEADS
RET_WIDTH = RET_HEADS * RET_DV
RET_K_SCALE = RET_DK ** -0.5
RET_CHUNK = 128
ROPE_THETA = 10000.0
EPS = 1e-6
NEG_INF = -1e30
IN_SIZES = (Q_LORA, KV_LORA, QK_ROPE, RET_HEADS * RET_DK, RET_HEADS * RET_DK,
            RET_WIDTH, RET_WIDTH, MLA_WIDTH, RET_WIDTH)

LANES = 128
VMEM_LIMIT = 56 * 1024 * 1024


def _params(*sem):
    return pltpu.CompilerParams(dimension_semantics=sem, vmem_limit_bytes=VMEM_LIMIT)


def _sigmoid(x):
    return 0.5 * jnp.tanh(0.5 * x) + 0.5


def _dot(a, b):
    return jnp.dot(a, b, preferred_element_type=F32)


def _dot_nt(a, b):
    return lax.dot_general(a, b, (((1,), (1,)), ((), ())), preferred_element_type=F32)


def _dot_tn(a, b):
    return lax.dot_general(a, b, (((0,), (0,)), ((), ())), preferred_element_type=F32)


def _rmsnorm_kernel(x_ref, g_ref, o_ref):
    x = x_ref[...]
    ms = jnp.mean(x * x, axis=-1, keepdims=True)
    o_ref[...] = (x * lax.rsqrt(ms + EPS) * g_ref[...]).astype(o_ref.dtype)


def rmsnorm_rows(x, g, tm):
    m, d = x.shape
    return pl.pallas_call(
        _rmsnorm_kernel,
        grid=(m // tm,),
        in_specs=[pl.BlockSpec((tm, d), lambda i: (i, 0)),
                  pl.BlockSpec((1, d), lambda i: (0, 0))],
        out_specs=pl.BlockSpec((tm, d), lambda i: (i, 0)),
        out_shape=jax.ShapeDtypeStruct((m, d), BF16),
        compiler_params=_params("parallel"),
        name="rmsnorm_rows",
    )(x, g.reshape(1, d))


def _mm_kernel(a_ref, w_ref, o_ref):
    o_ref[...] = _dot(a_ref[...], w_ref[...]).astype(o_ref.dtype)


def matmul(a, w, out_dtype, tm, tn):
    m, k = a.shape
    n = w.shape[1]
    return pl.pallas_call(
        _mm_kernel,
        grid=(m // tm, n // tn),
        in_specs=[pl.BlockSpec((tm, k), lambda i, j: (i, 0)),
                  pl.BlockSpec((k, tn), lambda i, j: (0, j))],
        out_specs=pl.BlockSpec((tm, tn), lambda i, j: (i, j)),
        out_shape=jax.ShapeDtypeStruct((m, n), out_dtype),
        compiler_params=_params("parallel", "parallel"),
        name="matmul",
    )(a, w)


def _mm_rope_kernel(a_ref, w_ref, cos_ref, sin_ref, o_ref, *, n_q_tiles, heads_per_tile):
    acc = _dot(a_ref[...], w_ref[...])
    c = cos_ref[...]
    s = sin_ref[...]
    scale = jnp.where(pl.program_id(1) >= n_q_tiles, RET_K_SCALE, 1.0).astype(F32)
    half = RET_DK // 2
    for h in range(heads_per_tile):
        lo = h * RET_DK
        x1 = acc[:, lo:lo + half]
        x2 = acc[:, lo + half:lo + RET_DK]
        o_ref[:, lo:lo + half] = ((x1 * c - x2 * s) * scale).astype(o_ref.dtype)
        o_ref[:, lo + half:lo + RET_DK] = ((x1 * s + x2 * c) * scale).astype(o_ref.dtype)


def matmul_rope(a, w, cos, sin, out_dtype, tm, tn):
    m, k = a.shape
    n = w.shape[1]
    pos_blocks = cos.shape[0] // tm
    kern = functools.partial(_mm_rope_kernel, n_q_tiles=(n // 2) // tn,
                             heads_per_tile=tn // RET_DK)
    return pl.pallas_call(
        kern,
        grid=(m // tm, n // tn),
        in_specs=[pl.BlockSpec((tm, k), lambda i, j: (i, 0)),
                  pl.BlockSpec((k, tn), lambda i, j: (0, j)),
                  pl.BlockSpec((tm, RET_DK // 2), lambda i, j: (i % pos_blocks, 0)),
                  pl.BlockSpec((tm, RET_DK // 2), lambda i, j: (i % pos_blocks, 0))],
        out_specs=pl.BlockSpec((tm, tn), lambda i, j: (i, j)),
        out_shape=jax.ShapeDtypeStruct((m, n), out_dtype),
        compiler_params=_params("parallel", "parallel"),
        name="matmul_rope",
    )(a, w, cos, sin)


def _latent_kernel(a_ref, w_ref, qn_ref, kvn_ref, cos_ref, sin_ref,
                   cq_ref, ckv_ref, kpe_ref, kcat_ref):
    acc = _dot(a_ref[...], w_ref[...])
    cq = acc[:, :Q_LORA]
    cq = cq * lax.rsqrt(jnp.mean(cq * cq, axis=-1, keepdims=True) + EPS) * qn_ref[...]
    cq_ref[...] = cq.astype(cq_ref.dtype)
    ckv = acc[:, Q_LORA:Q_LORA + KV_LORA]
    ckv = ckv * lax.rsqrt(jnp.mean(ckv * ckv, axis=-1, keepdims=True) + EPS) * kvn_ref[...]
    ckv_ref[...] = ckv
    base = Q_LORA + KV_LORA
    x = acc[:, base:base + LANES]
    xs = acc[:, base + LANES:base + 2 * LANES]
    kpe = (x * cos_ref[...] + xs * sin_ref[...])[:, :QK_ROPE]
    kpe_ref[...] = kpe
    kcat_ref[:, :KV_LORA] = ckv.astype(kcat_ref.dtype)
    kcat_ref[:, KV_LORA:K_CAT] = kpe.astype(kcat_ref.dtype)


def latent_proj(a, w, q_norm, kv_norm, cos, sin, tm):
    m, k = a.shape
    n = w.shape[1]
    pos_blocks = cos.shape[0] // tm
    row = lambda i: (i, 0)
    return pl.pallas_call(
        _latent_kernel,
        grid=(m // tm,),
        in_specs=[pl.BlockSpec((tm, k), row),
                  pl.BlockSpec((k, n), lambda i: (0, 0)),
                  pl.BlockSpec((1, Q_LORA), lambda i: (0, 0)),
                  pl.BlockSpec((1, KV_LORA), lambda i: (0, 0)),
                  pl.BlockSpec((tm, LANES), lambda i: (i % pos_blocks, 0)),
                  pl.BlockSpec((tm, LANES), lambda i: (i % pos_blocks, 0))],
        out_specs=[pl.BlockSpec((tm, Q_LORA), row),
                   pl.BlockSpec((tm, KV_LORA), row),
                   pl.BlockSpec((tm, QK_ROPE), row),
                   pl.BlockSpec((tm, K_CAT), row)],
        out_shape=[jax.ShapeDtypeStruct((m, Q_LORA), BF16),
                   jax.ShapeDtypeStruct((m, KV_LORA), F32),
                   jax.ShapeDtypeStruct((m, QK_ROPE), F32),
                   jax.ShapeDtypeStruct((m, K_CAT), BF16)],
        compiler_params=_params("parallel"),
        name="latent_proj",
    )(a, w, q_norm.reshape(1, Q_LORA), kv_norm.reshape(1, KV_LORA), cos, sin)


def _query_kernel(cq_ref, wq_ref, wuk_ref, cos_ref, sin_ref, o_ref):
    q = _dot(cq_ref[...], wq_ref[...])
    for h in range(MLA_HEADS):
        qn = q[:, h * QK_NOPE:(h + 1) * QK_NOPE].astype(BF16)
        o_ref[h, :, :KV_LORA] = (_dot(qn, wuk_ref[h]) * SCORE_SCALE_LOG2).astype(o_ref.dtype)
    c = cos_ref[...] * SCORE_SCALE_LOG2
    s = sin_ref[...] * SCORE_SCALE_LOG2
    rope0 = MLA_HEADS * QK_NOPE
    swap0 = rope0 + MLA_HEADS * QK_ROPE
    for p in range(MLA_HEADS * QK_ROPE // LANES):
        x = q[:, rope0 + p * LANES:rope0 + (p + 1) * LANES]
        xs = q[:, swap0 + p * LANES:swap0 + (p + 1) * LANES]
        qp = x * c + xs * s
        o_ref[2 * p, :, KV_LORA:K_CAT] = qp[:, :QK_ROPE].astype(o_ref.dtype)
        o_ref[2 * p + 1, :, KV_LORA:K_CAT] = qp[:, QK_ROPE:].astype(o_ref.dtype)


def query_proj(cq, wq, wuk_t, cos, sin, out_dtype, tm):
    m = cq.shape[0]
    pos_blocks = cos.shape[0] // tm
    return pl.pallas_call(
        _query_kernel,
        grid=(m // tm,),
        in_specs=[pl.BlockSpec((tm, Q_LORA), lambda i: (i, 0)),
                  pl.BlockSpec(wq.shape, lambda i: (0, 0)),
                  pl.BlockSpec(wuk_t.shape, lambda i: (0, 0, 0)),
                  pl.BlockSpec((tm, LANES), lambda i: (i % pos_blocks, 0)),
                  pl.BlockSpec((tm, LANES), lambda i: (i % pos_blocks, 0))],
        out_specs=pl.BlockSpec((MLA_HEADS, tm, K_CAT), lambda i: (0, i, 0)),
        out_shape=jax.ShapeDtypeStruct((MLA_HEADS, m, K_CAT), out_dtype),
        compiler_params=_params("parallel"),
        name="query_proj",
    )(cq, wq, wuk_t, cos, sin)


def _lane_partial_sum(p):
    out = p[:, :LANES]
    for c in range(1, p.shape[1] // LANES):
        out = out + p[:, c * LANES:(c + 1) * LANES]
    return out


def _softmax_update(s, v, m_prev, l_prev, acc_prev):
    m_new = jnp.maximum(m_prev, jnp.max(s, axis=-1, keepdims=True))
    corr = jnp.exp2(m_prev - m_new)
    p = jnp.exp2(s - m_new)
    l_new = l_prev * corr + _lane_partial_sum(p)
    acc_new = acc_prev * corr + _dot(p.astype(BF16), v)
    return m_new, l_new, acc_new


def _softmax_finish(l, acc):
    return acc / jnp.sum(l, axis=-1, keepdims=True)


PAIR_FIRST, PAIR_LAST = 1, 2
ATTN_GROUP_ROWS = 512


def _prompt_attn_kernel(it_ref, jt_ref, fl_ref, q_ref, k_ref, o_ref, m_ref, l_ref, acc_ref,
                        *, tq, tk):
    p = pl.program_id(1)
    i = it_ref[p]
    j = jt_ref[p]
    flags = fl_ref[p]
    hg = max(1, ATTN_GROUP_ROWS // tq)
    n_groups = MLA_HEADS // hg
    rows = hg * tq

    @pl.when((flags & PAIR_FIRST) != 0)
    def _():
        m_ref[...] = jnp.full(m_ref.shape, NEG_INF, F32)
        l_ref[...] = jnp.zeros(l_ref.shape, F32)
        acc_ref[...] = jnp.zeros(acc_ref.shape, F32)

    def step(masked, nk=tk):
        k = k_ref[:nk]
        v = k[:, :KV_LORA]
        if masked:
            qpos = i * tq + lax.broadcasted_iota(jnp.int32, (tq, nk), 0)
            kpos = j * tk + lax.broadcasted_iota(jnp.int32, (tq, nk), 1)
            keep = (kpos <= qpos)[None]
        def scores(g):
            q = q_ref[g * hg:(g + 1) * hg].reshape(rows, K_CAT)
            s = _dot_nt(q, k)
            if masked:
                s = jnp.where(keep, s.reshape(hg, tq, nk), NEG_INF).reshape(rows, nk)
            return s

        s_next = scores(0)
        for g in range(n_groups):
            rs = slice(g * rows, (g + 1) * rows)
            s = s_next
            if g + 1 < n_groups:
                s_next = scores(g + 1)
            m_new, l_new, acc_new = _softmax_update(s, v, m_ref[rs], l_ref[rs], acc_ref[rs])
            m_ref[rs] = m_new
            l_ref[rs] = l_new
            acc_ref[rs] = acc_new

    variant = flags >> 2

    @pl.when(variant == 0)
    def _():
        step(False)

    for n in range(1, tk // tq + 1):
        @pl.when(variant == n)
        def _(n=n):
            step(True, n * tq)

    @pl.when((flags & PAIR_LAST) != 0)
    def _():
        o = _softmax_finish(l_ref[...], acc_ref[...])
        o_ref[...] = o.reshape(MLA_HEADS, tq, KV_LORA).astype(o_ref.dtype)


def prompt_attention(q_hm, kcat, batch, seq, tq, tk):
    nq = seq // tq
    nk = seq // tk
    rows = MLA_HEADS * tq
    assert tk % tq == 0
    it, jt, fl = [], [], []
    for i in range(nq):
        last_j = (i * tq + tq - 1) // tk
        for j in range(last_j + 1):
            masked = j * tk + tk - 1 > i * tq
            variant = (i * tq + tq - j * tk) // tq if masked else 0
            it.append(i)
            jt.append(j)
            fl.append((PAIR_FIRST if j == 0 else 0) | (PAIR_LAST if j == last_j else 0)
                      | (variant << 2))
    tables = [jnp.asarray(t, jnp.int32) for t in (it, jt, fl)]
    q_map = lambda b, p, it, jt, fl: (0, b * nq + it[p], 0)
    kern = functools.partial(_prompt_attn_kernel, tq=tq, tk=tk)
    grid_spec = pltpu.PrefetchScalarGridSpec(
        num_scalar_prefetch=3,
        grid=(batch, len(it)),
        in_specs=[pl.BlockSpec((MLA_HEADS, tq, K_CAT), q_map),
                  pl.BlockSpec((tk, K_CAT), lambda b, p, it, jt, fl: (b * nk + jt[p], 0))],
        out_specs=pl.BlockSpec((MLA_HEADS, tq, KV_LORA), q_map),
        scratch_shapes=[pltpu.VMEM((rows, 1), F32),
                        pltpu.VMEM((rows, LANES), F32),
                        pltpu.VMEM((rows, KV_LORA), F32)],
    )
    return pl.pallas_call(
        kern,
        grid_spec=grid_spec,
        out_shape=jax.ShapeDtypeStruct((MLA_HEADS, batch * seq, KV_LORA), BF16),
        compiler_params=_params("parallel", "arbitrary"),
        name="prompt_attention",
    )(*tables, q_hm, kcat)


RING_SLOTS = 2


def _sample_attn_kernel(pt_ref, q_ref, ckv_hbm, kpe_hbm, ckv_new_ref, kpe_new_ref, o_ref,
                        cring_ref, rring_ref, sem, cbuf_ref, rbuf_ref, knew_ref,
                        m_ref, l_ref, acc_ref, *, group, pages_per_step, t_new):
    j = pl.program_id(1)
    n_j = pl.num_programs(1)
    step = pl.program_id(0) * n_j + j
    n_steps = pl.num_programs(0) * n_j
    slot = lax.rem(step, RING_SLOTS)
    rows = MLA_HEADS * t_new

    def page_copies(at_step, at_slot, g):
        b_at = lax.div(at_step, n_j)
        j_at = lax.rem(at_step, n_j)
        out = []
        for u in range(pages_per_step):
            page = pt_ref[b_at * group + g, j_at * pages_per_step + u]
            idx = g * pages_per_step + u
            out.append(pltpu.make_async_copy(ckv_hbm.at[page], cring_ref.at[at_slot, idx],
                                             sem.at[at_slot, g]))
            out.append(pltpu.make_async_copy(kpe_hbm.at[page], rring_ref.at[at_slot, idx],
                                             sem.at[at_slot, g]))
        return out

    def start_all(copies):
        for n, c in enumerate(copies):
            c.start(priority=(n // 2 + n) % 2)

    @pl.when(step == 0)
    def _():
        for g in range(group):
            start_all(page_copies(step, slot, g))

    @pl.when(j == 0)
    def _():
        m_ref[...] = jnp.full(m_ref.shape, NEG_INF, F32)
        l_ref[...] = jnp.zeros(l_ref.shape, F32)
        acc_ref[...] = jnp.zeros(acc_ref.shape, F32)

    def query(g):
        return q_ref[:, g * t_new:(g + 1) * t_new, :].reshape(rows, K_CAT).astype(BF16)

    def scores(g):
        @pl.when(step + 1 < n_steps)
        def _():
            start_all(page_copies(step + 1, RING_SLOTS - 1 - slot, g))

        for c in page_copies(step, slot, g):
            c.wait()
        for u in range(pages_per_step):
            ps = slice(u * PAGE_SIZE, (u + 1) * PAGE_SIZE)
            idx = g * pages_per_step + u
            cbuf_ref[g, ps, :] = cring_ref[slot, idx].astype(BF16)
            rbuf_ref[g, :, ps] = rring_ref[slot, idx].astype(BF16)
        q = query(g)
        return _dot_nt(q[:, :KV_LORA], cbuf_ref[g]) + _dot(q[:, KV_LORA:], rbuf_ref[g])

    s_next = scores(0)
    for g in range(group):
        s = s_next
        if g + 1 < group:
            s_next = scores(g + 1)
        m_new, l_new, acc_new = _softmax_update(s, cbuf_ref[g], m_ref[g], l_ref[g], acc_ref[g])
        m_ref[g] = m_new
        l_ref[g] = l_new
        acc_ref[g] = acc_new

    @pl.when(j == pl.num_programs(1) - 1)
    def _():
        qpos = lax.broadcasted_iota(jnp.int32, (t_new, PAGE_SIZE), 0)
        kpos = lax.broadcasted_iota(jnp.int32, (t_new, PAGE_SIZE), 1)
        keep = (kpos <= qpos)[None]
        for g in range(group):
            ts = slice(g * t_new, (g + 1) * t_new)
            knew_ref[...] = jnp.zeros(knew_ref.shape, F32)
            knew_ref[:t_new, :KV_LORA] = ckv_new_ref[ts, :]
            knew_ref[:t_new, KV_LORA:K_CAT] = kpe_new_ref[ts, :]
            kn = knew_ref[...].astype(BF16)
            sn = _dot_nt(query(g), kn)
            sn = jnp.where(keep, sn.reshape(MLA_HEADS, t_new, PAGE_SIZE), NEG_INF)
            _, l_new, acc_new = _softmax_update(sn.reshape(rows, PAGE_SIZE), kn[:, :KV_LORA],
                                                m_ref[g], l_ref[g], acc_ref[g])
            o = _softmax_finish(l_new, acc_new)
            o_ref[:, ts, :] = o.reshape(MLA_HEADS, t_new, KV_LORA).astype(o_ref.dtype)


def sample_attention(q_hm, cache_ckv, cache_kpe_t, page_table, ckv_new, kpe_new, t_new,
                     group, pages_per_step):
    dec_batch, n_pages = page_table.shape
    n_steps = n_pages // pages_per_step
    rows = MLA_HEADS * t_new
    gt = group * t_new

    in_specs = [pl.BlockSpec((MLA_HEADS, gt, K_CAT), lambda b, j, pt: (0, b, 0)),
                pl.BlockSpec(memory_space=pl.ANY),
                pl.BlockSpec(memory_space=pl.ANY),
                pl.BlockSpec((gt, KV_LORA), lambda b, j, pt: (b, 0)),
                pl.BlockSpec((gt, QK_ROPE), lambda b, j, pt: (b, 0))]
    kern = functools.partial(_sample_attn_kernel, group=group,
                             pages_per_step=pages_per_step, t_new=t_new)
    keys = pages_per_step * PAGE_SIZE
    n_slots = group * pages_per_step
    grid_spec = pltpu.PrefetchScalarGridSpec(
        num_scalar_prefetch=1,
        grid=(dec_batch // group, n_steps),
        in_specs=in_specs,
        out_specs=pl.BlockSpec((MLA_HEADS, gt, KV_LORA), lambda b, j, pt: (0, b, 0)),
        scratch_shapes=[pltpu.VMEM((RING_SLOTS, n_slots, PAGE_SIZE, KV_LORA), F32),
                        pltpu.VMEM((RING_SLOTS, n_slots, QK_ROPE, PAGE_SIZE), F32),
                        pltpu.SemaphoreType.DMA((RING_SLOTS, group)),
                        pltpu.VMEM((group, keys, KV_LORA), BF16),
                        pltpu.VMEM((group, QK_ROPE, keys), BF16),
                        pltpu.VMEM((PAGE_SIZE, K_CAT), F32),
                        pltpu.VMEM((group, rows, 1), F32),
                        pltpu.VMEM((group, rows, LANES), F32),
                        pltpu.VMEM((group, rows, KV_LORA), F32)],
    )
    return pl.pallas_call(
        kern,
        grid_spec=grid_spec,
        out_shape=jax.ShapeDtypeStruct((MLA_HEADS, dec_batch * t_new, KV_LORA), F32),
        compiler_params=_params("arbitrary", "arbitrary"),
        name="sample_attention",
    )(page_table, q_hm, cache_ckv, cache_kpe_t, ckv_new, kpe_new)


def _uv_kernel(o_ref, w_ref, out_ref):
    for h in range(MLA_HEADS):
        out_ref[:, h * V_HEAD:(h + 1) * V_HEAD] = _dot(o_ref[h].astype(BF16),
                                                       w_ref[h]).astype(out_ref.dtype)


def value_up_proj(o_lat, w_uv, out_dtype, tm):
    heads, m, _ = o_lat.shape
    return pl.pallas_call(
        _uv_kernel,
        grid=(m // tm,),
        in_specs=[pl.BlockSpec((heads, tm, KV_LORA), lambda i: (0, i, 0)),
                  pl.BlockSpec((heads, KV_LORA, V_HEAD), lambda i: (0, 0, 0))],
        out_specs=pl.BlockSpec((tm, heads * V_HEAD), lambda i: (i, 0)),
        out_shape=jax.ShapeDtypeStruct((m, heads * V_HEAD), out_dtype),
        compiler_params=_params("parallel"),
        name="value_up_proj",
    )(o_lat, w_uv)


def _retention_kernel(*refs, chunk, has_s0):
    if has_s0:
        (q_ref, k_ref, v_ref, rg_ref, gm_ref, gr_ref, om_ref, rn_ref, s0_ref,
         mix_ref, sout_ref, s_ref) = refs
    else:
        (q_ref, k_ref, v_ref, rg_ref, gm_ref, gr_ref, om_ref, rn_ref,
         mix_ref, sout_ref, s_ref) = refs
    c = pl.program_id(1)

    @pl.when(c == 0)
    def _():
        if has_s0:
            s_ref[...] = s0_ref[0]
        else:
            s_ref[...] = jnp.zeros(s_ref.shape, F32)

    ii = lax.broadcasted_iota(jnp.int32, (chunk, chunk), 0)
    jj = lax.broadcasted_iota(jnp.int32, (chunk, chunk), 1)
    diff = (ii - jj).astype(F32)
    idx = lax.broadcasted_iota(jnp.int32, (chunk, 1), 0).astype(F32)
    for h in range(RET_HEADS):
        lg = math.log(1.0 - 2.0 ** (-5.0 - h))
        dmat = jnp.where(diff >= 0, jnp.exp(jnp.maximum(diff, 0.0) * lg), 0.0)
        q_dec = jnp.exp((idx + 1.0) * lg)
        k_dec = jnp.exp((chunk - 1.0 - idx) * lg)
        c_dec = math.exp(chunk * lg)
        sl = slice(h * RET_DK, (h + 1) * RET_DK)
        q = q_ref[:, sl].astype(BF16)
        k = k_ref[:, sl]
        v = v_ref[:, sl].astype(BF16)
        state = s_ref[h]
        a = _dot_nt(q, k.astype(BF16)) * dmat
        o = _dot(a.astype(BF16), v) + _dot(q, state.astype(BF16)) * q_dec
        kd = (k.astype(F32) * k_dec).astype(BF16)
        s_ref[h] = state * c_dec + _dot_tn(kd, v)
        mu = jnp.mean(o, axis=-1, keepdims=True)
        d = o - mu
        var = jnp.mean(d * d, axis=-1, keepdims=True)
        on = d * lax.rsqrt(var + EPS) * rn_ref[:, sl]
        rg = rg_ref[:, sl].astype(F32)
        o_ret = rg * _sigmoid(rg) * on
        mixed = (_sigmoid(gm_ref[:, sl].astype(F32)) * om_ref[:, sl].astype(F32)
                 + _sigmoid(gr_ref[:, sl].astype(F32)) * o_ret)
        mix_ref[:, sl] = mixed.astype(mix_ref.dtype)

    @pl.when(c == pl.num_programs(1) - 1)
    def _():
        sout_ref[0] = s_ref[...]


def retention_mix(qk, v, gates, o_mla, ret_norm, s0, batch, n_chunks, chunk, mix_dtype):
    m = v.shape[0]
    has_s0 = s0 is not None
    row = lambda col: (lambda b, c: (b * n_chunks + c, col))
    blk = lambda col: pl.BlockSpec((chunk, RET_WIDTH), row(col))
    state_spec = pl.BlockSpec((1, RET_HEADS, RET_DK, RET_DV), lambda b, c: (b, 0, 0, 0))
    in_specs = [blk(0), blk(1), blk(0), blk(0), blk(1), blk(2), blk(0),
                pl.BlockSpec((1, RET_WIDTH), lambda b, c: (0, 0))]
    args = [qk, qk, v, gates, gates, gates, o_mla, ret_norm.reshape(1, RET_WIDTH)]
    if has_s0:
        in_specs.append(state_spec)
        args.append(s0)
    kern = functools.partial(_retention_kernel, chunk=chunk, has_s0=has_s0)
    return pl.pallas_call(
        kern,
        grid=(batch, n_chunks),
        in_specs=in_specs,
        out_specs=[blk(0), state_spec],
        out_shape=[jax.ShapeDtypeStruct((m, RET_WIDTH), mix_dtype),
                   jax.ShapeDtypeStruct((batch, RET_HEADS, RET_DK, RET_DV), F32)],
        scratch_shapes=[pltpu.VMEM((RET_HEADS, RET_DK, RET_DV), F32)],
        compiler_params=_params("parallel", "arbitrary"),
        name="retention_mix",
    )(*args)


def _out_proj_kernel(mix_ref, w_ref, x_ref, gpost_ref, gpre_ref, x1_ref, f_ref):
    y = _dot(mix_ref[...].astype(BF16), w_ref[...])
    y = y * lax.rsqrt(jnp.mean(y * y, axis=-1, keepdims=True) + EPS) * gpost_ref[...]
    x1 = x_ref[...] + y
    x1_ref[...] = x1
    f = x1 * lax.rsqrt(jnp.mean(x1 * x1, axis=-1, keepdims=True) + EPS) * gpre_ref[...]
    f_ref[...] = f.astype(f_ref.dtype)


def out_proj(mixed, w_out, x, g_post, g_pre, tm):
    m, d = x.shape
    row = lambda i: (i, 0)
    vec = pl.BlockSpec((1, d), lambda i: (0, 0))
    return pl.pallas_call(
        _out_proj_kernel,
        grid=(m // tm,),
        in_specs=[pl.BlockSpec((tm, d), row), pl.BlockSpec((d, d), lambda i: (0, 0)),
                  pl.BlockSpec((tm, d), row), vec, vec],
        out_specs=[pl.BlockSpec((tm, d), row), pl.BlockSpec((tm, d), row)],
        out_shape=[jax.ShapeDtypeStruct((m, d), F32), jax.ShapeDtypeStruct((m, d), BF16)],
        compiler_params=_params("parallel"),
        name="out_proj",
    )(mixed, w_out, x, g_post.reshape(1, d), g_pre.reshape(1, d))


def _ffn_kernel(f_ref, wg_ref, wu_ref, wd_ref, x_ref, g_ref, o_ref):
    kk = pl.program_id(1)

    @pl.when(kk == 0)
    def _():
        o_ref[...] = jnp.zeros(o_ref.shape, F32)

    f = f_ref[...]
    g = _dot(f, wg_ref[...])
    u = _dot(f, wu_ref[...])
    act = (g * _sigmoid(g) * u).astype(BF16)
    o_ref[...] += _dot(act, wd_ref[...])

    @pl.when(kk == pl.num_programs(1) - 1)
    def _():
        y = o_ref[...]
        y = y * lax.rsqrt(jnp.mean(y * y, axis=-1, keepdims=True) + EPS) * g_ref[...]
        o_ref[...] = x_ref[...] + y


def ffn(f, w_gate, w_up, w_down, x1, g_post, tm, tf):
    m, d = f.shape
    dff = w_gate.shape[1]
    return pl.pallas_call(
        _ffn_kernel,
        grid=(m // tm, dff // tf),
        in_specs=[pl.BlockSpec((tm, d), lambda i, k: (i, 0)),
                  pl.BlockSpec((d, tf), lambda i, k: (0, k)),
                  pl.BlockSpec((d, tf), lambda i, k: (0, k)),
                  pl.BlockSpec((tf, d), lambda i, k: (k, 0)),
                  pl.BlockSpec((tm, d), lambda i, k: (i, 0)),
                  pl.BlockSpec((1, d), lambda i, k: (0, 0))],
        out_specs=pl.BlockSpec((tm, d), lambda i, k: (i, 0)),
        out_shape=jax.ShapeDtypeStruct((m, d), F32),
        compiler_params=_params("parallel", "arbitrary"),
        name="ffn",
    )(f, w_gate, w_up, w_down, x1, g_post.reshape(1, d))


def _rope_tables(pos, half):
    inv = ROPE_THETA ** (-jnp.arange(half, dtype=F32) / half)
    ang = pos.astype(F32)[:, None] * inv[None, :]
    return jnp.cos(ang), jnp.sin(ang)


def _tile_rows(table, rows):
    reps = max(1, rows // table.shape[0])
    return jnp.tile(table, (reps, 1))


def _prep_weights(w_in, w_uq, w_uk, w_uv, w_out, w_gate, w_up, w_down):
    offs = [0]
    for sz in IN_SIZES:
        offs.append(offs[-1] + sz)
    col = lambda a, b: w_in[:, offs[a]:offs[b]]
    k_pe = col(2, 3)
    half = QK_ROPE // 2
    k_pe_swap = jnp.concatenate([k_pe[:, half:], k_pe[:, :half]], axis=1)
    zpad = jnp.zeros((w_in.shape[0], LANES - QK_ROPE), w_in.dtype)
    w_lat = jnp.concatenate([col(0, 2), k_pe, zpad, k_pe_swap, zpad], axis=1)
    q_nope = w_uq[:, :, :QK_NOPE].reshape(Q_LORA, MLA_HEADS * QK_NOPE)
    q_rope = w_uq[:, :, QK_NOPE:]
    q_rope_swap = jnp.concatenate([q_rope[:, :, half:], q_rope[:, :, :half]], axis=2)
    w_q = jnp.concatenate([q_nope, q_rope.reshape(Q_LORA, -1), q_rope_swap.reshape(Q_LORA, -1)], axis=1)
    return {
        'w_qk': col(3, 5).astype(BF16),
        'w_v': col(5, 6).astype(BF16),
        'w_gates': col(6, 9).astype(BF16),
        'w_lat': w_lat.astype(BF16),
        'w_q': w_q.astype(BF16),
        'w_uk_t': jnp.swapaxes(w_uk, 1, 2).astype(BF16),
        'w_uv': w_uv.astype(BF16),
        'w_out': w_out.astype(BF16),
        'w_gate': w_gate.astype(BF16),
        'w_up': w_up.astype(BF16),
        'w_down': w_down.astype(BF16),
    }


def _row_tile(m, cap):
    t = min(m, cap)
    assert m % t == 0
    return t


def _layer(x, pos, attend, s0, batch, chunk, narrow_dtype, w, norms):
    m = x.shape[0]
    t_len = m // batch
    tm = _row_tile(m, 1024)
    tm_small = _row_tile(m, 512)

    cos_r, sin_r = _rope_tables(pos, RET_DK // 2)
    cos_r, sin_r = _tile_rows(cos_r, tm), _tile_rows(sin_r, tm)
    cos_m, sin_m = _rope_tables(pos, QK_ROPE // 2)
    reps = LANES // (QK_ROPE // 2)
    cos_m = _tile_rows(jnp.tile(cos_m, (1, reps)), tm_small)
    sin_m = _tile_rows(jnp.tile(jnp.concatenate([-sin_m, sin_m], axis=1), (1, reps // 2)), tm_small)

    h = rmsnorm_rows(x, norms['norm_mix_pre'], tm_small)
    qk = matmul_rope(h, w['w_qk'], cos_r, sin_r, narrow_dtype, tm, 512)
    v = matmul(h, w['w_v'], narrow_dtype, tm, 512)
    gates = matmul(h, w['w_gates'], narrow_dtype, tm, 512)
    cq, ckv, kpe, kcat = latent_proj(h, w['w_lat'], norms['q_norm'], norms['kv_norm'],
                                     cos_m, sin_m, tm_small)
    q_hm = query_proj(cq, w['w_q'], w['w_uk_t'], cos_m, sin_m, narrow_dtype, tm_small)
    o_lat = attend(q_hm, kcat, ckv, kpe)
    o_mla = value_up_proj(o_lat, w['w_uv'], narrow_dtype, _row_tile(m, 256))
    mixed, s_new = retention_mix(qk, v, gates, o_mla, norms['ret_norm'], s0,
                                 batch, t_len // chunk, chunk, narrow_dtype)
    x1, f = out_proj(mixed, w['w_out'], x, norms['norm_mix_post'], norms['norm_ffn_pre'],
                     tm_small)
    y = ffn(f, w['w_gate'], w['w_up'], w['w_down'], x1, norms['norm_ffn_post'], tm_small, 512)
    return y, ckv, kpe, s_new


def kernel(x_prompt, x_sample, cache_kv_latent, cache_k_rope, page_table, state_retention,
           w_in, q_norm, kv_norm, w_uq, w_uk, w_uv, ret_norm, w_out,
           norm_mix_pre, norm_mix_post, norm_ffn_pre, norm_ffn_post, w_gate, w_up, w_down):
    b, s_len, d = x_prompt.shape
    db, t_s, _ = x_sample.shape
    depth = w_in.shape[0]
    past_len = page_table.shape[1] * PAGE_SIZE
    pos_p = jnp.arange(s_len, dtype=jnp.int32)
    pos_s = past_len + jnp.arange(t_s, dtype=jnp.int32)
    pages_per_step = math.gcd(page_table.shape[1], 8)
    group = math.gcd(db, 4)

    yp = x_prompt.reshape(b * s_len, d)
    ys = x_sample.reshape(db * t_s, d)
    outs = [[] for _ in range(6)]
    for l in range(depth):
        w = _prep_weights(w_in[l], w_uq[l], w_uk[l], w_uv[l], w_out[l], w_gate[l], w_up[l], w_down[l])
        norms = {'q_norm': q_norm[l], 'kv_norm': kv_norm[l], 'ret_norm': ret_norm[l],
                 'norm_mix_pre': norm_mix_pre[l], 'norm_mix_post': norm_mix_post[l],
                 'norm_ffn_pre': norm_ffn_pre[l], 'norm_ffn_post': norm_ffn_post[l]}

        def attend_p(q_hm, kcat, ckv, kpe):
            return prompt_attention(q_hm, kcat, b, s_len, min(256, s_len), min(512, s_len))

        def attend_s(q_hm, kcat, ckv, kpe, l=l):
            cache_kpe_t = jnp.swapaxes(cache_k_rope[l], 1, 2)
            return sample_attention(q_hm, cache_kv_latent[l], cache_kpe_t, page_table,
                                    ckv, kpe, t_s, group, pages_per_step)

        yp, c1, k1, s1 = _layer(yp, pos_p, attend_p, None, b, min(RET_CHUNK, s_len), BF16, w, norms)
        ys, c2, k2, s2 = _layer(ys, pos_s, attend_s, state_retention[l], db, t_s, F32, w, norms)
        outs[0].append(c1.reshape(b, s_len, KV_LORA))
        outs[1].append(k1.reshape(b, s_len, QK_ROPE))
        outs[2].append(s1)
        outs[3].append(c2.reshape(db, t_s, KV_LORA))
        outs[4].append(k2.reshape(db, t_s, QK_ROPE))
        outs[5].append(s2)
    return (yp.reshape(b, s_len, d), ys.reshape(db, t_s, d)) + tuple(jnp.stack(o) for o in outs)
```

```python
import functools
import math

import jax
import jax.numpy as jnp
from jax import lax
from jax.experimental import pallas as pl
from jax.experimental.pallas import tpu as pltpu

F32 = jnp.float32
BF16 = jnp.bfloat16

D_MODEL = 2048
PAGE_SIZE = 128
MLA_HEADS = 16
Q_LORA = 512
KV_LORA = 512
QK_NOPE = 128
QK_ROPE = 64
V_HEAD = 128
MLA_WIDTH = MLA_HEADS * V_HEAD
MLA_SCALE = (QK_NOPE + QK_ROPE) ** -0.5
SCORE_SCALE_LOG2 = MLA_SCALE * math.log2(math.e)
K_CAT = KV_LORA + QK_ROPE
RET_HEADS = 8
RET_DK = D_MODEL // RET_HEADS
RET_DV = D_MODEL // RET_HEADS
RET_WIDTH = RET_HEADS * RET_DV
RET_K_SCALE = RET_DK ** -0.5
RET_CHUNK = 128
ROPE_THETA = 10000.0
EPS = 1e-6
NEG_INF = -1e30
IN_SIZES = (Q_LORA, KV_LORA, QK_ROPE, RET_HEADS * RET_DK, RET_HEADS * RET_DK,
            RET_WIDTH, RET_WIDTH, MLA_WIDTH, RET_WIDTH)

LANES = 128
VMEM_LIMIT = 56 * 1024 * 1024


def _params(*sem):
    return pltpu.CompilerParams(dimension_semantics=sem, vmem_limit_bytes=VMEM_LIMIT)


def _sigmoid(x):
    return 0.5 * jnp.tanh(0.5 * x) + 0.5


def _dot(a, b):
    return jnp.dot(a, b, preferred_element_type=F32)


def _dot_nt(a, b):
    return lax.dot_general(a, b, (((1,), (1,)), ((), ())), preferred_element_type=F32)


def _dot_tn(a, b):
    return lax.dot_general(a, b, (((0,), (0,)), ((), ())), preferred_element_type=F32)


def _rmsnorm_kernel(x_ref, g_ref, o_ref):
    x = x_ref[...]
    ms = jnp.mean(x * x, axis=-1, keepdims=True)
    o_ref[...] = (x * lax.rsqrt(ms + EPS) * g_ref[...]).astype(o_ref.dtype)


def rmsnorm_rows(x, g, tm):
    m, d = x.shape
    return pl.pallas_call(
        _rmsnorm_kernel,
        grid=(m // tm,),
        in_specs=[pl.BlockSpec((tm, d), lambda i: (i, 0)),
                  pl.BlockSpec((1, d), lambda i: (0, 0))],
        out_specs=pl.BlockSpec((tm, d), lambda i: (i, 0)),
        out_shape=jax.ShapeDtypeStruct((m, d), BF16),
        compiler_params=_params("parallel"),
        name="rmsnorm_rows",
    )(x, g.reshape(1, d))


def _mm_kernel(a_ref, w_ref, o_ref):
    o_ref[...] = _dot(a_ref[...], w_ref[...]).astype(o_ref.dtype)


def matmul(a, w, out_dtype, tm, tn):
    m, k = a.shape
    n = w.shape[1]
    return pl.pallas_call(
        _mm_kernel,
        grid=(m // tm, n // tn),
        in_specs=[pl.BlockSpec((tm, k), lambda i, j: (i, 0)),
                  pl.BlockSpec((k, tn), lambda i, j: (0, j))],
        out_specs=pl.BlockSpec((tm, tn), lambda i, j: (i, j)),
        out_shape=jax.ShapeDtypeStruct((m, n), out_dtype),
        compiler_params=_params("parallel", "parallel"),
        name="matmul",
    )(a, w)


def _mm_rope_kernel(a_ref, w_ref, cos_ref, sin_ref, o_ref, *, n_q_tiles, heads_per_tile):
    acc = _dot(a_ref[...], w_ref[...])
    c = cos_ref[...]
    s = sin_ref[...]
    scale = jnp.where(pl.program_id(1) >= n_q_tiles, RET_K_SCALE, 1.0).astype(F32)
    half = RET_DK // 2
    for h in range(heads_per_tile):
        lo = h * RET_DK
        x1 = acc[:, lo:lo + half]
        x2 = acc[:, lo + half:lo + RET_DK]
        o_ref[:, lo:lo + half] = ((x1 * c - x2 * s) * scale).astype(o_ref.dtype)
        o_ref[:, lo + half:lo + RET_DK] = ((x1 * s + x2 * c) * scale).astype(o_ref.dtype)


def matmul_rope(a, w, cos, sin, out_dtype, tm, tn):
    m, k = a.shape
    n = w.shape[1]
    pos_blocks = cos.shape[0] // tm
    kern = functools.partial(_mm_rope_kernel, n_q_tiles=(n // 2) // tn,
                             heads_per_tile=tn // RET_DK)
    return pl.pallas_call(
        kern,
        grid=(m // tm, n // tn),
        in_specs=[pl.BlockSpec((tm, k), lambda i, j: (i, 0)),
                  pl.BlockSpec((k, tn), lambda i, j: (0, j)),
                  pl.BlockSpec((tm, RET_DK // 2), lambda i, j: (i % pos_blocks, 0)),
                  pl.BlockSpec((tm, RET_DK // 2), lambda i, j: (i % pos_blocks, 0))],
        out_specs=pl.BlockSpec((tm, tn), lambda i, j: (i, j)),
        out_shape=jax.ShapeDtypeStruct((m, n), out_dtype),
        compiler_params=_params("parallel", "parallel"),
        name="matmul_rope",
    )(a, w, cos, sin)


def _latent_kernel(a_ref, w_ref, qn_ref, kvn_ref, cos_ref, sin_ref,
                   cq_ref, ckv_ref, kpe_ref, kcat_ref):
    acc = _dot(a_ref[...], w_ref[...])
    cq = acc[:, :Q_LORA]
    cq = cq * lax.rsqrt(jnp.mean(cq * cq, axis=-1, keepdims=True) + EPS) * qn_ref[...]
    cq_ref[...] = cq.astype(cq_ref.dtype)
    ckv = acc[:, Q_LORA:Q_LORA + KV_LORA]
    ckv = ckv * lax.rsqrt(jnp.mean(ckv * ckv, axis=-1, keepdims=True) + EPS) * kvn_ref[...]
    ckv_ref[...] = ckv
    base = Q_LORA + KV_LORA
    x = acc[:, base:base + LANES]
    xs = acc[:, base + LANES:base + 2 * LANES]
    kpe = (x * cos_ref[...] + xs * sin_ref[...])[:, :QK_ROPE]
    kpe_ref[...] = kpe
    kcat_ref[:, :KV_LORA] = ckv.astype(kcat_ref.dtype)
    kcat_ref[:, KV_LORA:K_CAT] = kpe.astype(kcat_ref.dtype)


def latent_proj(a, w, q_norm, kv_norm, cos, sin, tm):
    m, k = a.shape
    n = w.shape[1]
    pos_blocks = cos.shape[0] // tm
    row = lambda i: (i, 0)
    return pl.pallas_call(
        _latent_kernel,
        grid=(m // tm,),
        in_specs=[pl.BlockSpec((tm, k), row),
                  pl.BlockSpec((k, n), lambda i: (0, 0)),
                  pl.BlockSpec((1, Q_LORA), lambda i: (0, 0)),
                  pl.BlockSpec((1, KV_LORA), lambda i: (0, 0)),
                  pl.BlockSpec((tm, LANES), lambda i: (i % pos_blocks, 0)),
                  pl.BlockSpec((tm, LANES), lambda i: (i % pos_blocks, 0))],
        out_specs=[pl.BlockSpec((tm, Q_LORA), row),
                   pl.BlockSpec((tm, KV_LORA), row),
                   pl.BlockSpec((tm, QK_ROPE), row),
                   pl.BlockSpec((tm, K_CAT), row)],
        out_shape=[jax.ShapeDtypeStruct((m, Q_LORA), BF16),
                   jax.ShapeDtypeStruct((m, KV_LORA), F32),
                   jax.ShapeDtypeStruct((m, QK_ROPE), F32),
                   jax.ShapeDtypeStruct((m, K_CAT), BF16)],
        compiler_params=_params("parallel"),
        name="latent_proj",
    )(a, w, q_norm.reshape(1, Q_LORA), kv_norm.reshape(1, KV_LORA), cos, sin)


def _query_kernel(cq_ref, wq_ref, wuk_ref, cos_ref, sin_ref, o_ref):
    q = _dot(cq_ref[...], wq_ref[...])
    for h in range(MLA_HEADS):
        qn = q[:, h * QK_NOPE:(h + 1) * QK_NOPE].astype(BF16)
        o_ref[h, :, :KV_LORA] = (_dot(qn, wuk_ref[h]) * SCORE_SCALE_LOG2).astype(o_ref.dtype)
    c = cos_ref[...] * SCORE_SCALE_LOG2
    s = sin_ref[...] * SCORE_SCALE_LOG2
    rope0 = MLA_HEADS * QK_NOPE
    swap0 = rope0 + MLA_HEADS * QK_ROPE
    for p in range(MLA_HEADS * QK_ROPE // LANES):
        x = q[:, rope0 + p * LANES:rope0 + (p + 1) * LANES]
        xs = q[:, swap0 + p * LANES:swap0 + (p + 1) * LANES]
        qp = x * c + xs * s
        o_ref[2 * p, :, KV_LORA:K_CAT] = qp[:, :QK_ROPE].astype(o_ref.dtype)
        o_ref[2 * p + 1, :, KV_LORA:K_CAT] = qp[:, QK_ROPE:].astype(o_ref.dtype)


def query_proj(cq, wq, wuk_t, cos, sin, out_dtype, tm):
    m = cq.shape[0]
    pos_blocks = cos.shape[0] // tm
    return pl.pallas_call(
        _query_kernel,
        grid=(m // tm,),
        in_specs=[pl.BlockSpec((tm, Q_LORA), lambda i: (i, 0)),
                  pl.BlockSpec(wq.shape, lambda i: (0, 0)),
                  pl.BlockSpec(wuk_t.shape, lambda i: (0, 0, 0)),
                  pl.BlockSpec((tm, LANES), lambda i: (i % pos_blocks, 0)),
                  pl.BlockSpec((tm, LANES), lambda i: (i % pos_blocks, 0))],
        out_specs=pl.BlockSpec((MLA_HEADS, tm, K_CAT), lambda i: (0, i, 0)),
        out_shape=jax.ShapeDtypeStruct((MLA_HEADS, m, K_CAT), out_dtype),
        compiler_params=_params("parallel"),
        name="query_proj",
    )(cq, wq, wuk_t, cos, sin)


def _lane_partial_sum(p):
    out = p[:, :LANES]
    for c in range(1, p.shape[1] // LANES):
        out = out + p[:, c * LANES:(c + 1) * LANES]
    return out


def _softmax_update(s, v, m_prev, l_prev, acc_prev):
    m_new = jnp.maximum(m_prev, jnp.max(s, axis=-1, keepdims=True))
    corr = jnp.exp2(m_prev - m_new)
    p = jnp.exp2(s - m_new)
    l_new = l_prev * corr + _lane_partial_sum(p)
    acc_new = acc_prev * corr + _dot(p.astype(BF16), v)
    return m_new, l_new, acc_new


def _softmax_finish(l, acc):
    return acc / jnp.sum(l, axis=-1, keepdims=True)


PAIR_FIRST, PAIR_LAST = 1, 2
ATTN_GROUP_ROWS = 256
ATTN_SKEW = 2


def _prompt_attn_kernel(it_ref, jt_ref, fl_ref, q_ref, k_ref, o_ref, m_ref, l_ref, acc_ref,
                        *, tq, tk):
    p = pl.program_id(1)
    i = it_ref[p]
    j = jt_ref[p]
    flags = fl_ref[p]
    hg = max(1, ATTN_GROUP_ROWS // tq)
    n_groups = MLA_HEADS // hg
    rows = hg * tq

    @pl.when((flags & PAIR_FIRST) != 0)
    def _():
        m_ref[...] = jnp.full(m_ref.shape, NEG_INF, F32)
        l_ref[...] = jnp.zeros(l_ref.shape, F32)
        acc_ref[...] = jnp.zeros(acc_ref.shape, F32)

    def step(masked, nk=tk):
        k = k_ref[:nk]
        v = k[:, :KV_LORA]
        if masked:
            qpos = i * tq + lax.broadcasted_iota(jnp.int32, (tq, nk), 0)
            kpos = j * tk + lax.broadcasted_iota(jnp.int32, (tq, nk), 1)
            keep = (kpos <= qpos)[None]
        def scores(g):
            q = q_ref[g * hg:(g + 1) * hg].reshape(rows, K_CAT)
            s = _dot_nt(q, k)
            if masked:
                s = jnp.where(keep, s.reshape(hg, tq, nk), NEG_INF).reshape(rows, nk)
            return s

        pending = [scores(g) for g in range(min(ATTN_SKEW, n_groups))]
        for g in range(n_groups):
            rs = slice(g * rows, (g + 1) * rows)
            s = pending.pop(0)
            if g + ATTN_SKEW < n_groups:
                pending.append(scores(g + ATTN_SKEW))
            m_new, l_new, acc_new = _softmax_update(s, v, m_ref[rs], l_ref[rs], acc_ref[rs])
            m_ref[rs] = m_new
            l_ref[rs] = l_new
            acc_ref[rs] = acc_new

    variant = flags >> 2

    @pl.when(variant == 0)
    def _():
        step(False)

    for n in range(1, tk // tq + 1):
        @pl.when(variant == n)
        def _(n=n):
            step(True, n * tq)

    @pl.when((flags & PAIR_LAST) != 0)
    def _():
        o = _softmax_finish(l_ref[...], acc_ref[...])
        o_ref[...] = o.reshape(MLA_HEADS, tq, KV_LORA).astype(o_ref.dtype)


def prompt_attention(q_hm, kcat, batch, seq, tq, tk):
    nq = seq // tq
    nk = seq // tk
    rows = MLA_HEADS * tq
    assert tk % tq == 0
    it, jt, fl = [], [], []
    for i in range(nq):
        last_j = (i * tq + tq - 1) // tk
        for j in range(last_j + 1):
            masked = j * tk + tk - 1 > i * tq
            variant = (i * tq + tq - j * tk) // tq if masked else 0
            it.append(i)
            jt.append(j)
            fl.append((PAIR_FIRST if j == 0 else 0) | (PAIR_LAST if j == last_j else 0)
                      | (variant << 2))
    tables = [jnp.asarray(t, jnp.int32) for t in (it, jt, fl)]
    q_map = lambda b, p, it, jt, fl: (0, b * nq + it[p], 0)
    kern = functools.partial(_prompt_attn_kernel, tq=tq, tk=tk)
    grid_spec = pltpu.PrefetchScalarGridSpec(
        num_scalar_prefetch=3,
        grid=(batch, len(it)),
        in_specs=[pl.BlockSpec((MLA_HEADS, tq, K_CAT), q_map),
                  pl.BlockSpec((tk, K_CAT), lambda b, p, it, jt, fl: (b * nk + jt[p], 0))],
        out_specs=pl.BlockSpec((MLA_HEADS, tq, KV_LORA), q_map),
        scratch_shapes=[pltpu.VMEM((rows, 1), F32),
                        pltpu.VMEM((rows, LANES), F32),
                        pltpu.VMEM((rows, KV_LORA), F32)],
    )
    return pl.pallas_call(
        kern,
        grid_spec=grid_spec,
        out_shape=jax.ShapeDtypeStruct((MLA_HEADS, batch * seq, KV_LORA), BF16),
        compiler_params=_params("parallel", "arbitrary"),
        name="prompt_attention",
    )(*tables, q_hm, kcat)


RING_SLOTS = 2


def _sample_attn_kernel(pt_ref, q_ref, ckv_hbm, kpe_hbm, ckv_new_ref, kpe_new_ref, o_ref,
                        cring_ref, rring_ref, sem, cbuf_ref, rbuf_ref, knew_ref,
                        m_ref, l_ref, acc_ref, *, group, pages_per_step, t_new):
    j = pl.program_id(1)
    n_j = pl.num_programs(1)
    step = pl.program_id(0) * n_j + j
    n_steps = pl.num_programs(0) * n_j
    slot = lax.rem(step, RING_SLOTS)
    rows = MLA_HEADS * t_new

    def page_copies(at_step, at_slot, g):
        b_at = lax.div(at_step, n_j)
        j_at = lax.rem(at_step, n_j)
        out = []
        for u in range(pages_per_step):
            page = pt_ref[b_at * group + g, j_at * pages_per_step + u]
            idx = g * pages_per_step + u
            out.append(pltpu.make_async_copy(ckv_hbm.at[page], cring_ref.at[at_slot, idx],
                                             sem.at[at_slot, g]))
            out.append(pltpu.make_async_copy(kpe_hbm.at[page], rring_ref.at[at_slot, idx],
                                             sem.at[at_slot, g]))
        return out

    def start_all(copies):
        for n, c in enumerate(copies):
            c.start(priority=(n // 2 + n) % 2)

    @pl.when(step == 0)
    def _():
        for g in range(group):
            start_all(page_copies(step, slot, g))

    @pl.when(j == 0)
    def _():
        m_ref[...] = jnp.full(m_ref.shape, NEG_INF, F32)
        l_ref[...] = jnp.zeros(l_ref.shape, F32)
        acc_ref[...] = jnp.zeros(acc_ref.shape, F32)

    def query(g):
        return q_ref[:, g * t_new:(g + 1) * t_new, :].reshape(rows, K_CAT).astype(BF16)

    def scores(g):
        @pl.when(step + 1 < n_steps)
        def _():
            start_all(page_copies(step + 1, RING_SLOTS - 1 - slot, g))

        for c in page_copies(step, slot, g):
            c.wait()
        for u in range(pages_per_step):
            ps = slice(u * PAGE_SIZE, (u + 1) * PAGE_SIZE)
            idx = g * pages_per_step + u
            cbuf_ref[g, ps, :] = cring_ref[slot, idx].astype(BF16)
            rbuf_ref[g, :, ps] = rring_ref[slot, idx].astype(BF16)
        q = query(g)
        return _dot_nt(q[:, :KV_LORA], cbuf_ref[g]) + _dot(q[:, KV_LORA:], rbuf_ref[g])

    s_next = scores(0)
    for g in range(group):
        s = s_next
        if g + 1 < group:
            s_next = scores(g + 1)
        m_new, l_new, acc_new = _softmax_update(s, cbuf_ref[g], m_ref[g], l_ref[g], acc_ref[g])
        m_ref[g] = m_new
        l_ref[g] = l_new
        acc_ref[g] = acc_new

    @pl.when(j == pl.num_programs(1) - 1)
    def _():
        qpos = lax.broadcasted_iota(jnp.int32, (t_new, PAGE_SIZE), 0)
        kpos = lax.broadcasted_iota(jnp.int32, (t_new, PAGE_SIZE), 1)
        keep = (kpos <= qpos)[None]
        for g in range(group):
            ts = slice(g * t_new, (g + 1) * t_new)
            knew_ref[...] = jnp.zeros(knew_ref.shape, F32)
            knew_ref[:t_new, :KV_LORA] = ckv_new_ref[ts, :]
            knew_ref[:t_new, KV_LORA:K_CAT] = kpe_new_ref[ts, :]
            kn = knew_ref[...].astype(BF16)
            sn = _dot_nt(query(g), kn)
            sn = jnp.where(keep, sn.reshape(MLA_HEADS, t_new, PAGE_SIZE), NEG_INF)
            _, l_new, acc_new = _softmax_update(sn.reshape(rows, PAGE_SIZE), kn[:, :KV_LORA],
                                                m_ref[g], l_ref[g], acc_ref[g])
            o = _softmax_finish(l_new, acc_new)
            o_ref[:, ts, :] = o.reshape(MLA_HEADS, t_new, KV_LORA).astype(o_ref.dtype)


def sample_attention(q_hm, cache_ckv, cache_kpe_t, page_table, ckv_new, kpe_new, t_new,
                     group, pages_per_step):
    dec_batch, n_pages = page_table.shape
    n_steps = n_pages // pages_per_step
    rows = MLA_HEADS * t_new
    gt = group * t_new

    in_specs = [pl.BlockSpec((MLA_HEADS, gt, K_CAT), lambda b, j, pt: (0, b, 0)),
                pl.BlockSpec(memory_space=pl.ANY),
                pl.BlockSpec(memory_space=pl.ANY),
                pl.BlockSpec((gt, KV_LORA), lambda b, j, pt: (b, 0)),
                pl.BlockSpec((gt, QK_ROPE), lambda b, j, pt: (b, 0))]
    kern = functools.partial(_sample_attn_kernel, group=group,
                             pages_per_step=pages_per_step, t_new=t_new)
    keys = pages_per_step * PAGE_SIZE
    n_slots = group * pages_per_step
    grid_spec = pltpu.PrefetchScalarGridSpec(
        num_scalar_prefetch=1,
        grid=(dec_batch // group, n_steps),
        in_specs=in_specs,
        out_specs=pl.BlockSpec((MLA_HEADS, gt, KV_LORA), lambda b, j, pt: (0, b, 0)),
        scratch_shapes=[pltpu.VMEM((RING_SLOTS, n_slots, PAGE_SIZE, KV_LORA), F32),
                        pltpu.VMEM((RING_SLOTS, n_slots, QK_ROPE, PAGE_SIZE), F32),
                        pltpu.SemaphoreType.DMA((RING_SLOTS, group)),
                        pltpu.VMEM((group, keys, KV_LORA), BF16),
                        pltpu.VMEM((group, QK_ROPE, keys), BF16),
                        pltpu.VMEM((PAGE_SIZE, K_CAT), F32),
                        pltpu.VMEM((group, rows, 1), F32),
                        pltpu.VMEM((group, rows, LANES), F32),
                        pltpu.VMEM((group, rows, KV_LORA), F32)],
    )
    return pl.pallas_call(
        kern,
        grid_spec=grid_spec,
        out_shape=jax.ShapeDtypeStruct((MLA_HEADS, dec_batch * t_new, KV_LORA), F32),
        compiler_params=_params("arbitrary", "arbitrary"),
        name="sample_attention",
    )(page_table, q_hm, cache_ckv, cache_kpe_t, ckv_new, kpe_new)


def _uv_kernel(o_ref, w_ref, out_ref):
    for h in range(MLA_HEADS):
        out_ref[:, h * V_HEAD:(h + 1) * V_HEAD] = _dot(o_ref[h].astype(BF16),
                                                       w_ref[h]).astype(out_ref.dtype)


def value_up_proj(o_lat, w_uv, out_dtype, tm):
    heads, m, _ = o_lat.shape
    return pl.pallas_call(
        _uv_kernel,
        grid=(m // tm,),
        in_specs=[pl.BlockSpec((heads, tm, KV_LORA), lambda i: (0, i, 0)),
                  pl.BlockSpec((heads, KV_LORA, V_HEAD), lambda i: (0, 0, 0))],
        out_specs=pl.BlockSpec((tm, heads * V_HEAD), lambda i: (i, 0)),
        out_shape=jax.ShapeDtypeStruct((m, heads * V_HEAD), out_dtype),
        compiler_params=_params("parallel"),
        name="value_up_proj",
    )(o_lat, w_uv)


def _retention_kernel(*refs, chunk, has_s0):
    if has_s0:
        (q_ref, k_ref, v_ref, rg_ref, gm_ref, gr_ref, om_ref, rn_ref, s0_ref,
         mix_ref, sout_ref, s_ref) = refs
    else:
        (q_ref, k_ref, v_ref, rg_ref, gm_ref, gr_ref, om_ref, rn_ref,
         mix_ref, sout_ref, s_ref) = refs
    c = pl.program_id(1)

    @pl.when(c == 0)
    def _():
        if has_s0:
            s_ref[...] = s0_ref[0]
        else:
            s_ref[...] = jnp.zeros(s_ref.shape, F32)

    ii = lax.broadcasted_iota(jnp.int32, (chunk, chunk), 0)
    jj = lax.broadcasted_iota(jnp.int32, (chunk, chunk), 1)
    diff = (ii - jj).astype(F32)
    idx = lax.broadcasted_iota(jnp.int32, (chunk, 1), 0).astype(F32)
    for h in range(RET_HEADS):
        lg = math.log(1.0 - 2.0 ** (-5.0 - h))
        dmat = jnp.where(diff >= 0, jnp.exp(jnp.maximum(diff, 0.0) * lg), 0.0)
        q_dec = jnp.exp((idx + 1.0) * lg)
        k_dec = jnp.exp((chunk - 1.0 - idx) * lg)
        c_dec = math.exp(chunk * lg)
        sl = slice(h * RET_DK, (h + 1) * RET_DK)
        q = q_ref[:, sl].astype(BF16)
        k = k_ref[:, sl]
        v = v_ref[:, sl].astype(BF16)
        state = s_ref[h]
        a = _dot_nt(q, k.astype(BF16)) * dmat
        o = _dot(a.astype(BF16), v) + _dot(q, state.astype(BF16)) * q_dec
        kd = (k.astype(F32) * k_dec).astype(BF16)
        s_ref[h] = state * c_dec + _dot_tn(kd, v)
        mu = jnp.mean(o, axis=-1, keepdims=True)
        d = o - mu
        var = jnp.mean(d * d, axis=-1, keepdims=True)
        on = d * lax.rsqrt(var + EPS) * rn_ref[:, sl]
        rg = rg_ref[:, sl].astype(F32)
        o_ret = rg * _sigmoid(rg) * on
        mixed = (_sigmoid(gm_ref[:, sl].astype(F32)) * om_ref[:, sl].astype(F32)
                 + _sigmoid(gr_ref[:, sl].astype(F32)) * o_ret)
        mix_ref[:, sl] = mixed.astype(mix_ref.dtype)

    @pl.when(c == pl.num_programs(1) - 1)
    def _():
        sout_ref[0] = s_ref[...]


def retention_mix(qk, v, gates, o_mla, ret_norm, s0, batch, n_chunks, chunk, mix_dtype):
    m = v.shape[0]
    has_s0 = s0 is not None
    row = lambda col: (lambda b, c: (b * n_chunks + c, col))
    blk = lambda col: pl.BlockSpec((chunk, RET_WIDTH), row(col))
    state_spec = pl.BlockSpec((1, RET_HEADS, RET_DK, RET_DV), lambda b, c: (b, 0, 0, 0))
    in_specs = [blk(0), blk(1), blk(0), blk(0), blk(1), blk(2), blk(0),
                pl.BlockSpec((1, RET_WIDTH), lambda b, c: (0, 0))]
    args = [qk, qk, v, gates, gates, gates, o_mla, ret_norm.reshape(1, RET_WIDTH)]
    if has_s0:
        in_specs.append(state_spec)
        args.append(s0)
    kern = functools.partial(_retention_kernel, chunk=chunk, has_s0=has_s0)
    return pl.pallas_call(
        kern,
        grid=(batch, n_chunks),
        in_specs=in_specs,
        out_specs=[blk(0), state_spec],
        out_shape=[jax.ShapeDtypeStruct((m, RET_WIDTH), mix_dtype),
                   jax.ShapeDtypeStruct((batch, RET_HEADS, RET_DK, RET_DV), F32)],
        scratch_shapes=[pltpu.VMEM((RET_HEADS, RET_DK, RET_DV), F32)],
        compiler_params=_params("parallel", "arbitrary"),
        name="retention_mix",
    )(*args)


def _out_proj_kernel(mix_ref, w_ref, x_ref, gpost_ref, gpre_ref, x1_ref, f_ref):
    y = _dot(mix_ref[...].astype(BF16), w_ref[...])
    y = y * lax.rsqrt(jnp.mean(y * y, axis=-1, keepdims=True) + EPS) * gpost_ref[...]
    x1 = x_ref[...] + y
    x1_ref[...] = x1
    f = x1 * lax.rsqrt(jnp.mean(x1 * x1, axis=-1, keepdims=True) + EPS) * gpre_ref[...]
    f_ref[...] = f.astype(f_ref.dtype)


def out_proj(mixed, w_out, x, g_post, g_pre, tm):
    m, d = x.shape
    row = lambda i: (i, 0)
    vec = pl.BlockSpec((1, d), lambda i: (0, 0))
    return pl.pallas_call(
        _out_proj_kernel,
        grid=(m // tm,),
        in_specs=[pl.BlockSpec((tm, d), row), pl.BlockSpec((d, d), lambda i: (0, 0)),
                  pl.BlockSpec((tm, d), row), vec, vec],
        out_specs=[pl.BlockSpec((tm, d), row), pl.BlockSpec((tm, d), row)],
        out_shape=[jax.ShapeDtypeStruct((m, d), F32), jax.ShapeDtypeStruct((m, d), BF16)],
        compiler_params=_params("parallel"),
        name="out_proj",
    )(mixed, w_out, x, g_post.reshape(1, d), g_pre.reshape(1, d))


def _ffn_kernel(f_ref, wg_ref, wu_ref, wd_ref, x_ref, g_ref, o_ref):
    kk = pl.program_id(1)

    @pl.when(kk == 0)
    def _():
        o_ref[...] = jnp.zeros(o_ref.shape, F32)

    f = f_ref[...]
    g = _dot(f, wg_ref[...])
    u = _dot(f, wu_ref[...])
    act = (g * _sigmoid(g) * u).astype(BF16)
    o_ref[...] += _dot(act, wd_ref[...])

    @pl.when(kk == pl.num_programs(1) - 1)
    def _():
        y = o_ref[...]
        y = y * lax.rsqrt(jnp.mean(y * y, axis=-1, keepdims=True) + EPS) * g_ref[...]
        o_ref[...] = x_ref[...] + y


def ffn(f, w_gate, w_up, w_down, x1, g_post, tm, tf):
    m, d = f.shape
    dff = w_gate.shape[1]
    return pl.pallas_call(
        _ffn_kernel,
        grid=(m // tm, dff // tf),
        in_specs=[pl.BlockSpec((tm, d), lambda i, k: (i, 0)),
                  pl.BlockSpec((d, tf), lambda i, k: (0, k)),
                  pl.BlockSpec((d, tf), lambda i, k: (0, k)),
                  pl.BlockSpec((tf, d), lambda i, k: (k, 0)),
                  pl.BlockSpec((tm, d), lambda i, k: (i, 0)),
                  pl.BlockSpec((1, d), lambda i, k: (0, 0))],
        out_specs=pl.BlockSpec((tm, d), lambda i, k: (i, 0)),
        out_shape=jax.ShapeDtypeStruct((m, d), F32),
        compiler_params=_params("parallel", "arbitrary"),
        name="ffn",
    )(f, w_gate, w_up, w_down, x1, g_post.reshape(1, d))


def _rope_tables(pos, half):
    inv = ROPE_THETA ** (-jnp.arange(half, dtype=F32) / half)
    ang = pos.astype(F32)[:, None] * inv[None, :]
    return jnp.cos(ang), jnp.sin(ang)


def _tile_rows(table, rows):
    reps = max(1, rows // table.shape[0])
    return jnp.tile(table, (reps, 1))


def _prep_weights(w_in, w_uq, w_uk, w_uv, w_out, w_gate, w_up, w_down):
    offs = [0]
    for sz in IN_SIZES:
        offs.append(offs[-1] + sz)
    col = lambda a, b: w_in[:, offs[a]:offs[b]]
    k_pe = col(2, 3)
    half = QK_ROPE // 2
    k_pe_swap = jnp.concatenate([k_pe[:, half:], k_pe[:, :half]], axis=1)
    zpad = jnp.zeros((w_in.shape[0], LANES - QK_ROPE), w_in.dtype)
    w_lat = jnp.concatenate([col(0, 2), k_pe, zpad, k_pe_swap, zpad], axis=1)
    q_nope = w_uq[:, :, :QK_NOPE].reshape(Q_LORA, MLA_HEADS * QK_NOPE)
    q_rope = w_uq[:, :, QK_NOPE:]
    q_rope_swap = jnp.concatenate([q_rope[:, :, half:], q_rope[:, :, :half]], axis=2)
    w_q = jnp.concatenate([q_nope, q_rope.reshape(Q_LORA, -1), q_rope_swap.reshape(Q_LORA, -1)], axis=1)
    return {
        'w_qk': col(3, 5).astype(BF16),
        'w_v': col(5, 6).astype(BF16),
        'w_gates': col(6, 9).astype(BF16),
        'w_lat': w_lat.astype(BF16),
        'w_q': w_q.astype(BF16),
        'w_uk_t': jnp.swapaxes(w_uk, 1, 2).astype(BF16),
        'w_uv': w_uv.astype(BF16),
        'w_out': w_out.astype(BF16),
        'w_gate': w_gate.astype(BF16),
        'w_up': w_up.astype(BF16),
        'w_down': w_down.astype(BF16),
    }


def _row_tile(m, cap):
    t = min(m, cap)
    assert m % t == 0
    return t


def _layer(x, pos, attend, s0, batch, chunk, narrow_dtype, w, norms):
    m = x.shape[0]
    t_len = m // batch
    tm = _row_tile(m, 1024)
    tm_small = _row_tile(m, 512)

    cos_r, sin_r = _rope_tables(pos, RET_DK // 2)
    cos_r, sin_r = _tile_rows(cos_r, tm), _tile_rows(sin_r, tm)
    cos_m, sin_m = _rope_tables(pos, QK_ROPE // 2)
    reps = LANES // (QK_ROPE // 2)
    cos_m = _tile_rows(jnp.tile(cos_m, (1, reps)), tm_small)
    sin_m = _tile_rows(jnp.tile(jnp.concatenate([-sin_m, sin_m], axis=1), (1, reps // 2)), tm_small)

    h = rmsnorm_rows(x, norms['norm_mix_pre'], tm_small)
    qk = matmul_rope(h, w['w_qk'], cos_r, sin_r, narrow_dtype, tm, 512)
    v = matmul(h, w['w_v'], narrow_dtype, tm, 512)
    gates = matmul(h, w['w_gates'], narrow_dtype, tm, 512)
    cq, ckv, kpe, kcat = latent_proj(h, w['w_lat'], norms['q_norm'], norms['kv_norm'],
                                     cos_m, sin_m, tm_small)
    q_hm = query_proj(cq, w['w_q'], w['w_uk_t'], cos_m, sin_m, narrow_dtype, tm_small)
    o_lat = attend(q_hm, kcat, ckv, kpe)
    o_mla = value_up_proj(o_lat, w['w_uv'], narrow_dtype, _row_tile(m, 256))
    mixed, s_new = retention_mix(qk, v, gates, o_mla, norms['ret_norm'], s0,
                                 batch, t_len // chunk, chunk, narrow_dtype)
    x1, f = out_proj(mixed, w['w_out'], x, norms['norm_mix_post'], norms['norm_ffn_pre'],
                     tm_small)
    y = ffn(f, w['w_gate'], w['w_up'], w['w_down'], x1, norms['norm_ffn_post'], tm_small, 512)
    return y, ckv, kpe, s_new


def kernel(x_prompt, x_sample, cache_kv_latent, cache_k_rope, page_table, state_retention,
           w_in, q_norm, kv_norm, w_uq, w_uk, w_uv, ret_norm, w_out,
           norm_mix_pre, norm_mix_post, norm_ffn_pre, norm_ffn_post, w_gate, w_up, w_down):
    b, s_len, d = x_prompt.shape
    db, t_s, _ = x_sample.shape
    depth = w_in.shape[0]
    past_len = page_table.shape[1] * PAGE_SIZE
    pos_p = jnp.arange(s_len, dtype=jnp.int32)
    pos_s = past_len + jnp.arange(t_s, dtype=jnp.int32)
    pages_per_step = math.gcd(page_table.shape[1], 8)
    group = math.gcd(db, 4)

    yp = x_prompt.reshape(b * s_len, d)
    ys = x_sample.reshape(db * t_s, d)
    outs = [[] for _ in range(6)]
    for l in range(depth):
        w = _prep_weights(w_in[l], w_uq[l], w_uk[l], w_uv[l], w_out[l], w_gate[l], w_up[l], w_down[l])
        norms = {'q_norm': q_norm[l], 'kv_norm': kv_norm[l], 'ret_norm': ret_norm[l],
                 'norm_mix_pre': norm_mix_pre[l], 'norm_mix_post': norm_mix_post[l],
                 'norm_ffn_pre': norm_ffn_pre[l], 'norm_ffn_post': norm_ffn_post[l]}

        def attend_p(q_hm, kcat, ckv, kpe):
            return prompt_attention(q_hm, kcat, b, s_len, min(256, s_len), min(512, s_len))

        def attend_s(q_hm, kcat, ckv, kpe, l=l):
            cache_kpe_t = jnp.swapaxes(cache_k_rope[l], 1, 2)
            return sample_attention(q_hm, cache_kv_latent[l], cache_kpe_t, page_table,
                                    ckv, kpe, t_s, group, pages_per_step)

        yp, c1, k1, s1 = _layer(yp, pos_p, attend_p, None, b, min(RET_CHUNK, s_len), BF16, w, norms)
        ys, c2, k2, s2 = _layer(ys, pos_s, attend_s, state_retention[l], db, t_s, F32, w, norms)
        outs[0].append(c1.reshape(b, s_len, KV_LORA))
        outs[1].append(k1.reshape(b, s_len, QK_ROPE))
        outs[2].append(s1)
        outs[3].append(c2.reshape(db, t_s, KV_LORA))
        outs[4].append(k2.reshape(db, t_s, QK_ROPE))
        outs[5].append(s2)
    return (yp.reshape(b, s_len, d), ys.reshape(db, t_s, d)) + tuple(jnp.stack(o) for o in outs)
```
